```python
import math
import jax
import jax.numpy as jnp
from jax import lax
import numpy as np

D_MODEL = 2048
BATCH = 8
SEQ = 4096
DEPTH = 2

N_EVEN = (DEPTH + 1) // 2
N_ODD = DEPTH // 2
EPS = 1e-6
NEG = -1e30

GDN_HEADS = 8
GDN_DK = 128
GDN_DV = 128
GDN_CONV = 5
GDN_CHUNK = 64
DSW_HEADS = 8
DSW_DIM = 128
DSW_PAIRS = ((128, 1), (512, 4), (2048, 16))
MLA_HEADS = 8
MLA_Q_RANK = 768
MLA_KV_RANK = 512
MLA_NOPE = 128
MLA_ROPE = 64
MLA_V = 128
MLA_QBLOCK = 128
ROPE_THETA = 10000.0
MLSTM_HEADS = 8
MLSTM_DQK = 64
MLSTM_DV = 128
MLSTM_CHUNK = 64
FFN_DIM = 7168
N_EXPERTS = 8
TOP_K = 2
EXPERT_DIM = 7168

GDN_QK = GDN_HEADS * GDN_DK
GDN_V = GDN_HEADS * GDN_DV
DSW_W = DSW_HEADS * DSW_DIM
L0_SIZES = (GDN_QK, GDN_QK, GDN_V, GDN_V, 2 * GDN_HEADS, 2 * GDN_HEADS, DSW_W, DSW_W, DSW_W)
L0_IN = sum(L0_SIZES)
L0_MIX = GDN_V + DSW_W
ML_QK = MLSTM_HEADS * MLSTM_DQK
ML_V = MLSTM_HEADS * MLSTM_DV
MLA_W = MLA_HEADS * MLA_V
L1_SIZES = (MLA_Q_RANK, MLA_KV_RANK, MLA_ROPE, ML_QK, ML_QK, ML_V, ML_V, 2 * MLSTM_HEADS, 2 * MLSTM_HEADS)
L1_IN = sum(L1_SIZES)
L1_MIX = MLA_W + ML_V

kernel_name = 'hybrid_gdn_dilated_mla_mlstm_moe_encoder'


def rmsnorm(x, g):
    xf = x.astype(jnp.float32)
    y = xf * lax.rsqrt(jnp.mean(xf * xf, axis=-1, keepdims=True) + EPS)
    return (y * g.astype(jnp.float32)).astype(x.dtype)


def l2norm(x):
    xf = x.astype(jnp.float32)
    return xf * lax.rsqrt(jnp.sum(xf * xf, axis=-1, keepdims=True) + EPS)


def split_cols(t, sizes):
    cuts = [int(c) for c in np.cumsum(sizes)[:-1]]
    return jnp.split(t, cuts, axis=-1)


def to_heads(t, n_heads):
    b, s, _ = t.shape
    return t.reshape(b, s, n_heads, -1).transpose(0, 2, 1, 3)


def from_heads(t):
    b, n, s, d = t.shape
    return t.transpose(0, 2, 1, 3).reshape(b, s, n * d)


def dir_heads(t, n_heads):
    b, s, _ = t.shape
    return t.astype(jnp.float32).reshape(b, s, 2, n_heads).transpose(2, 0, 3, 1)


def flip_seq(t):
    return jnp.flip(t, axis=2)


def swiglu(x, w_gate, w_up, w_down):
    return (jax.nn.silu(x @ w_gate) * (x @ w_up)) @ w_down


def short_conv(x, w):
    c = x.shape[-1]
    k = w.shape[0]
    return lax.conv_general_dilated(
        x, w[:, None, :].astype(x.dtype), window_strides=(1,), padding=[(k // 2, k // 2)],
        dimension_numbers=('NWC', 'WIO', 'NWC'), feature_group_count=c)


def gated_delta_rule(q, k, v, g, beta):
    b_, h_, s_, dk = q.shape
    dv = v.shape[-1]
    c_ = GDN_CHUNK
    n_ = s_ // c_
    q = q.reshape(b_, h_, n_, c_, dk)
    k = k.reshape(b_, h_, n_, c_, dk)
    v = v.reshape(b_, h_, n_, c_, dv)
    g = g.reshape(b_, h_, n_, c_)
    beta = beta.reshape(b_, h_, n_, c_)
    gc = jnp.cumsum(g, axis=-1)
    idx = jnp.arange(c_)
    incl = idx[:, None] >= idx[None, :]
    strict = idx[:, None] > idx[None, :]
    gam = jnp.exp(jnp.where(incl, gc[..., :, None] - gc[..., None, :], -jnp.inf))
    kb = k * beta[..., None]
    m = jnp.where(strict, jnp.einsum('bhncd,bhnsd->bhncs', kb, k) * gam, 0.0)
    a = m + jnp.eye(c_, dtype=jnp.float32)
    rhs = jnp.concatenate([v * beta[..., None], kb * jnp.exp(gc)[..., None]], axis=-1)
    sol = lax.linalg.triangular_solve(a, rhs, left_side=True, lower=True)
    u, w = sol[..., :dv], sol[..., dv:]
    qk = jnp.einsum('bhncd,bhnsd->bhncs', q, k) * gam
    qd = q * jnp.exp(gc)[..., None]
    glast = gc[..., -1]
    kd = k * jnp.exp(glast[..., None] - gc)[..., None]

    def step(state, inp):
        qd_i, qk_i, u_i, w_i, kd_i, gl_i = inp
        v_new = u_i - jnp.einsum('bhcd,bhde->bhce', w_i, state)
        o = jnp.einsum('bhcd,bhde->bhce', qd_i, state) + jnp.einsum('bhcs,bhse->bhce', qk_i, v_new)
        state = state * jnp.exp(gl_i)[..., None, None] + jnp.einsum('bhcd,bhce->bhde', kd_i, v_new)
        return state, o

    xs = tuple(jnp.moveaxis(t, 2, 0) for t in (qd, qk, u, w, kd, glast))
    state0 = jnp.zeros((b_, h_, dk, dv), jnp.float32)
    _, o = lax.scan(step, state0, xs)
    return jnp.moveaxis(o, 0, 2).reshape(b_, h_, s_, dv)


def dilated_branch(q, k, v, r, half, slopes):
    b_, h_, s_, d_ = q.shape
    u_ = s_ // r
    blk = half
    nb = -(-u_ // blk)
    up = nb * blk

    def to_class(t):
        t = t.reshape(b_, h_, u_, r, d_).transpose(0, 1, 3, 2, 4)
        return jnp.pad(t, ((0, 0), (0, 0), (0, 0), (0, up - u_), (0, 0)))

    def windows(t):
        tp = jnp.pad(to_class(t), ((0, 0), (0, 0), (0, 0), (blk, blk), (0, 0)))
        tp = tp.reshape(b_, h_, r, nb + 2, blk, d_)
        return jnp.concatenate([tp[:, :, :, :-2], tp[:, :, :, 1:-1], tp[:, :, :, 2:]], axis=-2)

    qc = to_class(q).reshape(b_, h_, r, nb, blk, d_)
    kw = windows(k)
    vw = windows(v)
    qpos = jnp.arange(up).reshape(nb, blk)
    kpos = jnp.arange(nb)[:, None] * blk - blk + jnp.arange(3 * blk)[None, :]
    delta = kpos[:, None, :] - qpos[:, :, None]
    valid = (jnp.abs(delta) <= half) & (kpos[:, None, :] >= 0) & (kpos[:, None, :] < u_)
    dist = (r * jnp.abs(delta)).astype(jnp.float32)
    s = jnp.einsum('bhrnqd,bhrnkd->bhrnqk', qc, kw).astype(jnp.float32) * (d_ ** -0.5)
    s = s - slopes[None, :, None, None, None, None] * dist
    s = jnp.where(valid, s, NEG)
    mx = jnp.max(s, axis=-1, keepdims=True)
    p = jnp.exp(s - mx)
    den = jnp.sum(p, axis=-1)
    o = jnp.einsum('bhrnqk,bhrnkd->bhrnqd', p, vw.astype(jnp.float32)) / den[..., None]
    lse = mx[..., 0] + jnp.log(den)

    def from_class(t):
        rest = t.shape[5:]
        t = t.reshape(b_, h_, r, up, *rest)[:, :, :, :u_]
        t = jnp.swapaxes(t, 2, 3)
        return t.reshape(b_, h_, s_, *rest)

    return from_class(o), from_class(lse)


def dilated_attention(q, k, v):
    h_ = q.shape[1]
    slopes = 2.0 ** (-8.0 * jnp.arange(1, h_ + 1, dtype=jnp.float32) / h_)
    outs, lses = [], []
    for window, r in DSW_PAIRS:
        o, l = dilated_branch(q, k, v, r, window // (2 * r), slopes)
        outs.append(o)
        lses.append(l)
    wts = jax.nn.softmax(jnp.stack(lses, axis=-1), axis=-1)
    return jnp.einsum('bhsjd,bhsj->bhsd', jnp.stack(outs, axis=3), wts)


def rope_tables(s_, dr):
    pos = jnp.arange(s_, dtype=jnp.float32)
    freqs = ROPE_THETA ** (-jnp.arange(0, dr, 2, dtype=jnp.float32) / dr)
    ang = pos[:, None] * freqs[None, :]
    return jnp.cos(ang), jnp.sin(ang)


def apply_rope(x, cos, sin):
    half = x.shape[-1] // 2
    x1 = x[..., :half].astype(jnp.float32)
    x2 = x[..., half:].astype(jnp.float32)
    return jnp.concatenate([x1 * cos - x2 * sin, x1 * sin + x2 * cos], axis=-1).astype(x.dtype)


def mla(cq, ckv, kr, q_norm, kv_norm, w_uq, w_ukv):
    b_, s_, _ = cq.shape
    q = (rmsnorm(cq, q_norm) @ w_uq).reshape(b_, s_, MLA_HEADS, MLA_NOPE + MLA_ROPE)
    kv = (rmsnorm(ckv, kv_norm) @ w_ukv).reshape(b_, s_, MLA_HEADS, MLA_NOPE + MLA_V)
    q_nope, q_rope = q[..., :MLA_NOPE], q[..., MLA_NOPE:]
    k_nope, v = kv[..., :MLA_NOPE], kv[..., MLA_NOPE:]
    cos, sin = rope_tables(s_, MLA_ROPE)
    q_rope = apply_rope(q_rope, cos[:, None, :], sin[:, None, :])
    k_rope = apply_rope(kr, cos, sin)
    scale = (MLA_NOPE + MLA_ROPE) ** -0.5
    nq = s_ // MLA_QBLOCK

    def block_attn(qs):
        qn, qr = qs
        s = jnp.einsum('bqhd,bkhd->bhqk', qn, k_nope) + jnp.einsum('bqhd,bkd->bhqk', qr, k_rope)
        p = jax.nn.softmax(s.astype(jnp.float32) * scale, axis=-1).astype(v.dtype)
        return jnp.einsum('bhqk,bkhd->bqhd', p, v)

    def to_blocks(t):
        return jnp.moveaxis(t.reshape(b_, nq, MLA_QBLOCK, *t.shape[2:]), 1, 0)

    o = lax.map(block_attn, (to_blocks(q_nope), to_blocks(q_rope)))
    return jnp.moveaxis(o, 0, 1).reshape(b_, s_, MLA_W)


def mlstm_chunkwise(q, k, v, ig, lf):
    b_, h_, s_, dk = q.shape
    dv = v.shape[-1]
    l_ = MLSTM_CHUNK
    n_ = s_ // l_
    q = q.reshape(b_, h_, n_, l_, dk)
    k = k.reshape(b_, h_, n_, l_, dk)
    v = v.reshape(b_, h_, n_, l_, dv)
    ig = ig.reshape(b_, h_, n_, l_)
    lf = lf.reshape(b_, h_, n_, l_)
    bcum = jnp.cumsum(lf, axis=-1)
    idx = jnp.arange(l_)
    incl = idx[:, None] >= idx[None, :]
    dmat = jnp.where(incl, bcum[..., :, None] - bcum[..., None, :] + ig[..., None, :], NEG)
    dmax = jnp.max(dmat, axis=-1)
    qk = jnp.einsum('bhnld,bhnsd->bhnls', q, k)
    blast = bcum[..., -1]
    wlog = blast[..., None] - bcum + ig
    wmax = jnp.max(wlog, axis=-1)

    def step(carry, inp):
        c, n, m = carry
        q_i, k_i, v_i, b_i, d_i, dmax_i, qk_i, bl_i, wl_i, wm_i = inp
        inter = b_i + m[..., None]
        mj = jnp.maximum(inter, dmax_i)
        s = qk_i * jnp.exp(d_i - mj[..., None])
        iw = jnp.exp(inter - mj)
        num = iw[..., None] * jnp.einsum('bhld,bhde->bhle', q_i, c) + jnp.einsum('bhls,bhse->bhle', s, v_i)
        den = iw * jnp.einsum('bhld,bhd->bhl', q_i, n) + jnp.sum(s, axis=-1)
        h = num / jnp.maximum(jnp.abs(den), jnp.exp(-mj))[..., None]
        m_new = jnp.maximum(bl_i + m, wm_i)
        ws = jnp.exp(wl_i - m_new[..., None])
        decay = jnp.exp(bl_i + m - m_new)
        c = decay[..., None, None] * c + jnp.einsum('bhld,bhle->bhde', k_i * ws[..., None], v_i)
        n = decay[..., None] * n + jnp.einsum('bhld,bhl->bhd', k_i, ws)
        return (c, n, m_new), h

    xs = tuple(jnp.moveaxis(t, 2, 0) for t in (q, k, v, bcum, dmat, dmax, qk, blast, wlog, wmax))
    init = (jnp.zeros((b_, h_, dk, dv), jnp.float32), jnp.zeros((b_, h_, dk), jnp.float32),
            jnp.zeros((b_, h_), jnp.float32))
    _, h = lax.scan(step, init, xs)
    return jnp.moveaxis(h, 0, 2).reshape(b_, h_, s_, dv)


def mixer_ab(xn, w_in, conv_w, a_log, dt_bias, gdn_norm, w_out):
    b_, s_, _ = xn.shape
    proj = xn @ w_in
    gq, gk, gv, gz, ga, gb, sq, sk, sv = split_cols(proj, L0_SIZES)
    qkv = jax.nn.silu(short_conv(jnp.concatenate([gq, gk, gv], axis=-1), conv_w))
    cq, ck, cv = split_cols(qkv, (GDN_QK, GDN_QK, GDN_V))
    q = l2norm(to_heads(cq, GDN_HEADS)) * (GDN_DK ** -0.5)
    k = l2norm(to_heads(ck, GDN_HEADS))
    v = to_heads(cv, GDN_HEADS).astype(jnp.float32)
    g = -jnp.exp(a_log.astype(jnp.float32))[:, None, :, None] * jax.nn.softplus(
        dir_heads(ga, GDN_HEADS) + dt_bias.astype(jnp.float32)[:, None, :, None])
    beta = jax.nn.sigmoid(dir_heads(gb, GDN_HEADS))
    o_fwd = gated_delta_rule(q, k, v, g[0], beta[0])
    o_bwd = flip_seq(gated_delta_rule(flip_seq(q), flip_seq(k), flip_seq(v), flip_seq(g[1]), flip_seq(beta[1])))
    o = (o_fwd + o_bwd).transpose(0, 2, 1, 3)
    z = gz.astype(jnp.float32).reshape(b_, s_, GDN_HEADS, GDN_DV)
    o_a = (rmsnorm(o, gdn_norm) * jax.nn.silu(z)).reshape(b_, s_, GDN_V)
    o_b = from_heads(dilated_attention(to_heads(sq, DSW_HEADS), to_heads(sk, DSW_HEADS), to_heads(sv, DSW_HEADS)))
    mixed = jnp.concatenate([o_a.astype(xn.dtype), o_b.astype(xn.dtype)], axis=-1)
    return mixed @ w_out


def mixer_cd(xn, w_in, q_norm, kv_norm, w_uq, w_ukv, ig_bias, fg_bias, mlstm_norm, w_out):
    b_, s_, _ = xn.shape
    proj = xn @ w_in
    cq, ckv, kr, mq, mk, mv, mo, mi, mf = split_cols(proj, L1_SIZES)
    o_c = mla(cq, ckv, kr, q_norm, kv_norm, w_uq, w_ukv)
    q = to_heads(mq, MLSTM_HEADS).astype(jnp.float32)
    k = to_heads(mk, MLSTM_HEADS).astype(jnp.float32) * (MLSTM_DQK ** -0.5)
    v = to_heads(mv, MLSTM_HEADS).astype(jnp.float32)
    ig = dir_heads(mi, MLSTM_HEADS) + ig_bias.astype(jnp.float32)[:, None, :, None]
    lf = jax.nn.log_sigmoid(dir_heads(mf, MLSTM_HEADS) + fg_bias.astype(jnp.float32)[:, None, :, None])
    h_fwd = mlstm_chunkwise(q, k, v, ig[0], lf[0])
    h_bwd = flip_seq(mlstm_chunkwise(flip_seq(q), flip_seq(k), flip_seq(v), flip_seq(ig[1]), flip_seq(lf[1])))
    h = (h_fwd + h_bwd).transpose(0, 2, 1, 3)
    og = jax.nn.sigmoid(mo.astype(jnp.float32)).reshape(b_, s_, MLSTM_HEADS, MLSTM_DV)
    o_d = (rmsnorm(h, mlstm_norm) * og).reshape(b_, s_, ML_V)
    mixed = jnp.concatenate([o_c.astype(xn.dtype), o_d.astype(xn.dtype)], axis=-1)
    return mixed @ w_out


def moe_swiglu(x, w_router, b_router, we_gate, we_up, we_down):
    b_, s_, d_ = x.shape
    xf = x.reshape(b_ * s_, d_)
    logits = (xf @ w_router).astype(jnp.float32) + b_router.astype(jnp.float32)
    probs = jax.nn.softmax(logits, axis=-1)
    topv, topi = lax.top_k(probs, TOP_K)
    topv = topv / jnp.sum(topv, axis=-1, keepdims=True)
    comb = jnp.einsum('tke,tk->te', jax.nn.one_hot(topi, N_EXPERTS, dtype=jnp.float32), topv)
    out = jnp.zeros_like(xf)
    for e in range(N_EXPERTS):
        out = out + comb[:, e:e + 1].astype(x.dtype) * swiglu(xf, we_gate[e], we_up[e], we_down[e])
    return out.reshape(b_, s_, d_)


def setup_inputs(seed: int = 0) -> dict:
    key = jax.random.key(seed)
    ks = iter(jax.random.split(key, 64))
    f32 = jnp.float32

    def nrm(shape, scale):
        return scale * jax.random.normal(next(ks), shape, f32)

    def gain(shape):
        return 1.0 + 0.02 * jax.random.normal(next(ks), shape, f32)

    ne, no = N_EVEN, N_ODD
    dt = jnp.exp(jax.random.uniform(next(ks), (ne, 2, GDN_HEADS), f32, math.log(1e-3), math.log(0.1)))
    return {
        'x': jax.random.normal(next(ks), (BATCH, SEQ, D_MODEL), f32),
        'even_mix_norm': gain((ne, D_MODEL)),
        'even_w_in': nrm((ne, D_MODEL, L0_IN), D_MODEL ** -0.5),
        'even_conv_w': nrm((ne, GDN_CONV, 2 * GDN_QK + GDN_V), GDN_CONV ** -0.5),
        'even_a_log': jnp.log(jax.random.uniform(next(ks), (ne, 2, GDN_HEADS), f32, 1.0, 16.0)),
        'even_dt_bias': dt + jnp.log(-jnp.expm1(-dt)),
        'even_gdn_norm': gain((ne, GDN_DV)),
        'even_w_out': nrm((ne, L0_MIX, D_MODEL), L0_MIX ** -0.5),
        'even_ffn_norm': gain((ne, D_MODEL)),
        'even_w_gate': nrm((ne, D_MODEL, FFN_DIM), D_MODEL ** -0.5),
        'even_w_up': nrm((ne, D_MODEL, FFN_DIM), D_MODEL ** -0.5),
        'even_w_down': nrm((ne, FFN_DIM, D_MODEL), FFN_DIM ** -0.5),
        'odd_mix_norm': gain((no, D_MODEL)),
        'odd_w_in': nrm((no, D_MODEL, L1_IN), D_MODEL ** -0.5),
        'odd_q_norm': gain((no, MLA_Q_RANK)),
        'odd_kv_norm': gain((no, MLA_KV_RANK)),
        'odd_w_uq': nrm((no, MLA_Q_RANK, MLA_HEADS * (MLA_NOPE + MLA_ROPE)), MLA_Q_RANK ** -0.5),
        'odd_w_ukv': nrm((no, MLA_KV_RANK, MLA_HEADS * (MLA_NOPE + MLA_V)), MLA_KV_RANK ** -0.5),
        'odd_ig_bias': nrm((no, 2, MLSTM_HEADS), 0.1),
        'odd_fg_bias': 3.0 + nrm((no, 2, MLSTM_HEADS), 0.5),
        'odd_mlstm_norm': gain((no, MLSTM_HEADS, MLSTM_DV)),
        'odd_w_out': nrm((no, L1_MIX, D_MODEL), L1_MIX ** -0.5),
        'odd_ffn_norm': gain((no, D_MODEL)),
        'odd_w_router': nrm((no, D_MODEL, N_EXPERTS), D_MODEL ** -0.5),
        'odd_b_router': nrm((no, N_EXPERTS), 0.01),
        'odd_we_gate': nrm((no, N_EXPERTS, D_MODEL, EXPERT_DIM), D_MODEL ** -0.5),
        'odd_we_up': nrm((no, N_EXPERTS, D_MODEL, EXPERT_DIM), D_MODEL ** -0.5),
        'odd_we_down': nrm((no, N_EXPERTS, EXPERT_DIM, D_MODEL), EXPERT_DIM ** -0.5),
        'final_norm': gain((D_MODEL,)),
    }


def reference(x, even_mix_norm, even_w_in, even_conv_w, even_a_log, even_dt_bias, even_gdn_norm,
              even_w_out, even_ffn_norm, even_w_gate, even_w_up, even_w_down,
              odd_mix_norm, odd_w_in, odd_q_norm, odd_kv_norm, odd_w_uq, odd_w_ukv,
              odd_ig_bias, odd_fg_bias, odd_mlstm_norm, odd_w_out, odd_ffn_norm,
              odd_w_router, odd_b_router, odd_we_gate, odd_we_up, odd_we_down, final_norm):
    h = x
    for layer in range(DEPTH):
        j = layer // 2
        if layer % 2 == 0:
            h = h + mixer_ab(rmsnorm(h, even_mix_norm[j]), even_w_in[j], even_conv_w[j], even_a_log[j],
                             even_dt_bias[j], even_gdn_norm[j], even_w_out[j])
            h = h + swiglu(rmsnorm(h, even_ffn_norm[j]), even_w_gate[j], even_w_up[j], even_w_down[j])
        else:
            h = h + mixer_cd(rmsnorm(h, odd_mix_norm[j]), odd_w_in[j], odd_q_norm[j], odd_kv_norm[j],
                             odd_w_uq[j], odd_w_ukv[j], odd_ig_bias[j], odd_fg_bias[j],
                             odd_mlstm_norm[j], odd_w_out[j])
            h = h + moe_swiglu(rmsnorm(h, odd_ffn_norm[j]), odd_w_router[j], odd_b_router[j],
                               odd_we_gate[j], odd_we_up[j], odd_we_down[j])
    return rmsnorm(h, final_norm)
```

```python
import functools
import math

import jax
import jax.numpy as jnp
import numpy as np
from jax import lax
from jax.experimental import pallas as pl
from jax.experimental.pallas import tpu as pltpu

F32 = jnp.float32
BF16 = jnp.bfloat16
EPS = 1e-6
NEG = -1e30
HI = lax.Precision.HIGHEST

LANES = 128
CHUNK = 64
VMEM_LIMIT = 56 * 1024 * 1024

GDN_HEADS = 8
GDN_CONV = 5
DSW_HEADS = 8
DSW_PAIRS = ((128, 1), (512, 4), (2048, 16))
MLA_HEADS = 8
MLA_Q_RANK = 768
MLA_KV_RANK = 512
MLA_NOPE = 128
MLA_ROPE = 64
MLA_V = 128
ROPE_THETA = 10000.0
MLSTM_HEADS = 8
MLSTM_DQK = 64
MLSTM_DV = 128
N_EXPERTS = 8


def _cparams(sem):
    return pltpu.CompilerParams(dimension_semantics=sem, vmem_limit_bytes=VMEM_LIMIT)


def _dot(a, b):
    return jnp.dot(a, b, preferred_element_type=F32)


def _dot_nt(a, b):
    return lax.dot_general(a, b, (((1,), (1,)), ((), ())), preferred_element_type=F32)


def _dot_tn(a, b):
    return lax.dot_general(a, b, (((0,), (0,)), ((), ())), preferred_element_type=F32)


def _dot_hi(a, b):
    return jnp.dot(a, b, preferred_element_type=F32, precision=HI)


def _nm_kernel(x_ref, g_ref, w_ref, o_ref, xn_ref, *, use_norm):
    @pl.when(pl.program_id(1) == 0)
    def _():
        x = x_ref[...].astype(F32)
        if use_norm:
            ms = jnp.mean(x * x, axis=-1, keepdims=True)
            x = x * lax.rsqrt(ms + EPS) * g_ref[...]
        xn_ref[...] = x.astype(BF16)

    o_ref[...] = _dot(xn_ref[...], w_ref[...]).astype(o_ref.dtype)


def norm_matmul(x, gain, w, *, out_dtype, k_block=0, k_width=None, tm=512, tn=512, use_norm=True, name="norm_matmul"):
    t = x.shape[0]
    kw = x.shape[1] if k_width is None else k_width
    n = w.shape[1]
    tm = min(tm, t)
    tn = min(tn, n)
    assert t % tm == 0 and n % tn == 0 and w.shape[0] == kw
    g2 = gain.reshape(1, kw).astype(F32)
    return pl.pallas_call(
        functools.partial(_nm_kernel, use_norm=use_norm),
        grid=(t // tm, n // tn),
        in_specs=[
            pl.BlockSpec((tm, kw), lambda i, j: (i, k_block)),
            pl.BlockSpec((1, kw), lambda i, j: (0, 0)),
            pl.BlockSpec((kw, tn), lambda i, j: (0, j)),
        ],
        out_specs=pl.BlockSpec((tm, tn), lambda i, j: (i, j)),
        out_shape=jax.ShapeDtypeStruct((t, n), out_dtype),
        scratch_shapes=[pltpu.VMEM((tm, kw), BF16)],
        compiler_params=_cparams(("parallel", "arbitrary")),
        name=name,
    )(x, g2, w)


def _proj_res_kernel(a_ref, b_ref, w1_ref, w2_ref, r_ref, o_ref):
    acc = _dot(a_ref[...], w1_ref[...]) + _dot(b_ref[...], w2_ref[...])
    o_ref[...] = r_ref[...] + acc


def proj_residual(a, b, w1, w2, res, *, tm=512, name="proj_residual"):
    t, ka = a.shape
    kb = b.shape[1]
    n = w1.shape[1]
    tm = min(tm, t)
    assert t % tm == 0
    return pl.pallas_call(
        _proj_res_kernel,
        grid=(t // tm,),
        in_specs=[
            pl.BlockSpec((tm, ka), lambda i: (i, 0)),
            pl.BlockSpec((tm, kb), lambda i: (i, 0)),
            pl.BlockSpec((ka, n), lambda i: (0, 0)),
            pl.BlockSpec((kb, n), lambda i: (0, 0)),
            pl.BlockSpec((tm, n), lambda i: (i, 0)),
        ],
        out_specs=pl.BlockSpec((tm, n), lambda i: (i, 0)),
        out_shape=jax.ShapeDtypeStruct((t, n), F32),
        compiler_params=_cparams(("parallel",)),
        name=name,
    )(a, b, w1, w2, res)


def _silu(a):
    return a * jax.nn.sigmoid(a)


def _ffn_kernel(h_ref, g_ref, wg_ref, wu_ref, wd_ref, o_ref, xn_ref, acc_ref):
    j = pl.program_id(1)

    @pl.when(j == 0)
    def _():
        x = h_ref[...]
        ms = jnp.mean(x * x, axis=-1, keepdims=True)
        xn_ref[...] = (x * lax.rsqrt(ms + EPS) * g_ref[...]).astype(BF16)
        acc_ref[...] = jnp.zeros_like(acc_ref)

    xn = xn_ref[...]
    mid = (_silu(_dot(xn, wg_ref[...])) * _dot(xn, wu_ref[...])).astype(BF16)
    acc_ref[...] += _dot(mid, wd_ref[...])

    @pl.when(j == pl.num_programs(1) - 1)
    def _():
        o_ref[...] = h_ref[...] + acc_ref[...]


def ffn_residual(h, gain, wg, wu, wd, *, tm=512, tf=512, name="ffn_swiglu"):
    t, d = h.shape
    f = wg.shape[1]
    tm = min(tm, t)
    tf = min(tf, f)
    assert t % tm == 0 and f % tf == 0
    return pl.pallas_call(
        _ffn_kernel,
        grid=(t // tm, f // tf),
        in_specs=[
            pl.BlockSpec((tm, d), lambda i, j: (i, 0)),
            pl.BlockSpec((1, d), lambda i, j: (0, 0)),
            pl.BlockSpec((d, tf), lambda i, j: (0, j)),
            pl.BlockSpec((d, tf), lambda i, j: (0, j)),
            pl.BlockSpec((tf, d), lambda i, j: (j, 0)),
        ],
        out_specs=pl.BlockSpec((tm, d), lambda i, j: (i, 0)),
        out_shape=jax.ShapeDtypeStruct((t, d), F32),
        scratch_shapes=[pltpu.VMEM((tm, d), BF16), pltpu.VMEM((tm, d), F32)],
        compiler_params=_cparams(("parallel", "arbitrary")),
        name=name,
    )(h, gain.reshape(1, d).astype(F32), wg, wu, wd)


def _router_kernel(h_ref, g_ref, wr_ref, br_ref, xn_ref, info_ref, *, n_experts):
    x = h_ref[...]
    ms = jnp.mean(x * x, axis=-1, keepdims=True)
    xn = x * lax.rsqrt(ms + EPS) * g_ref[...]
    xn_ref[...] = xn
    logits = _dot_hi(xn, wr_ref[...]) + br_ref[...]
    lane = lax.broadcasted_iota(jnp.int32, logits.shape, 1)
    real = lane < n_experts
    lg = jnp.where(real, logits, -jnp.inf)
    mx = jnp.max(lg, axis=-1, keepdims=True)
    ex = jnp.exp(lg - mx)
    probs = ex / jnp.sum(ex, axis=-1, keepdims=True)
    probs = jnp.where(real, probs, -1.0)
    v1 = jnp.max(probs, axis=-1, keepdims=True)
    i1 = jnp.min(jnp.where(probs == v1, lane, LANES), axis=-1, keepdims=True)
    rest = jnp.where(lane == i1, -1.0, probs)
    v2 = jnp.max(rest, axis=-1, keepdims=True)
    i2 = jnp.min(jnp.where(rest == v2, lane, LANES), axis=-1, keepdims=True)
    tot = v1 + v2
    info = jnp.where(lane == 0, v1 / tot, 0.0)
    info = jnp.where(lane == 1, v2 / tot, info)
    info = jnp.where(lane == 2, i1.astype(F32), info)
    info = jnp.where(lane == 3, i2.astype(F32), info)
    info_ref[...] = info


def router(h, gain, w_router, b_router, *, tm=512):
    t, d = h.shape
    e = w_router.shape[1]
    tm = min(tm, t)
    wr = jnp.zeros((d, LANES), F32).at[:, :e].set(w_router.astype(F32))
    br = jnp.zeros((1, LANES), F32).at[0, :e].set(b_router.astype(F32))
    return pl.pallas_call(
        functools.partial(_router_kernel, n_experts=e),
        grid=(t // tm,),
        in_specs=[
            pl.BlockSpec((tm, d), lambda i: (i, 0)),
            pl.BlockSpec((1, d), lambda i: (0, 0)),
            pl.BlockSpec((d, LANES), lambda i: (0, 0)),
            pl.BlockSpec((1, LANES), lambda i: (0, 0)),
        ],
        out_specs=[pl.BlockSpec((tm, d), lambda i: (i, 0)), pl.BlockSpec((tm, LANES), lambda i: (i, 0))],
        out_shape=[jax.ShapeDtypeStruct((t, d), F32), jax.ShapeDtypeStruct((t, LANES), F32)],
        compiler_params=_cparams(("parallel",)),
        name="router",
    )(h, gain.reshape(1, d).astype(F32), wr, br)


def _moe_kernel(te_ref, na_ref, nv_ref, tok_ref, dst_ref, xn_hbm, wg_ref, wu_ref, wd_ref, out_hbm,
                xrow_ref, xb_ref, acc_ref, sem_in, sem_out, *, tm):
    i = pl.program_id(0)
    j = pl.program_id(1)
    nj = pl.num_programs(1)
    active = i < na_ref[0]
    n_valid = nv_ref[i]

    def gather_copy(r):
        return pltpu.make_async_copy(xn_hbm.at[pl.ds(tok_ref[0, 0, r], 1)], xrow_ref.at[pl.ds(r, 1)], sem_in)

    def scatter_copy(r):
        return pltpu.make_async_copy(acc_ref.at[pl.ds(r, 1)], out_hbm.at[pl.ds(dst_ref[0, 0, r], 1)], sem_out)

    @pl.when(active & (j == 0))
    def _():
        def start(r, c):
            gather_copy(r).start()
            return c

        lax.fori_loop(0, tm, start, 0)

        def wait(r, c):
            gather_copy(r).wait()
            return c

        lax.fori_loop(0, tm, wait, 0)
        xb_ref[...] = xrow_ref[...].astype(BF16)
        acc_ref[...] = jnp.zeros_like(acc_ref)

    @pl.when(active)
    def _():
        xb = xb_ref[...]
        mid = (_silu(_dot(xb, wg_ref[0])) * _dot(xb, wu_ref[0])).astype(BF16)
        acc_ref[...] += _dot(mid, wd_ref[0])

    @pl.when(active & (j == nj - 1))
    def _():
        def start(r, c):
            scatter_copy(r).start()
            return c

        lax.fori_loop(0, n_valid, start, 0)

        def wait(r, c):
            scatter_copy(r).wait()
            return c

        lax.fori_loop(0, n_valid, wait, 0)


def moe_experts(xn, tile_expert, n_active, n_valid, tok_of_slot, dst_of_slot, wg, wu, wd, *, tm, tf):
    t, d = xn.shape
    e, _, f = wg.shape
    n_tiles = tok_of_slot.shape[0]
    tf = min(tf, f)
    nj = f // tf

    def w_in_map(i, j, te, na, nv):
        return (te[i], 0, jnp.where(i < na[0], j, nj - 1))

    def w_dn_map(i, j, te, na, nv):
        return (te[i], jnp.where(i < na[0], j, nj - 1), 0)

    grid_spec = pltpu.PrefetchScalarGridSpec(
        num_scalar_prefetch=3,
        grid=(n_tiles, nj),
        in_specs=[
            pl.BlockSpec((1, 1, tm), lambda i, j, te, na, nv: (i, 0, 0), memory_space=pltpu.SMEM),
            pl.BlockSpec((1, 1, tm), lambda i, j, te, na, nv: (i, 0, 0), memory_space=pltpu.SMEM),
            pl.BlockSpec(memory_space=pl.ANY),
            pl.BlockSpec((1, d, tf), w_in_map),
            pl.BlockSpec((1, d, tf), w_in_map),
            pl.BlockSpec((1, tf, d), w_dn_map),
        ],
        out_specs=pl.BlockSpec(memory_space=pl.ANY),
        scratch_shapes=[
            pltpu.VMEM((tm, d), F32),
            pltpu.VMEM((tm, d), BF16),
            pltpu.VMEM((tm, d), F32),
            pltpu.SemaphoreType.DMA(()),
            pltpu.SemaphoreType.DMA(()),
        ],
    )
    return pl.pallas_call(
        functools.partial(_moe_kernel, tm=tm),
        grid_spec=grid_spec,
        out_shape=jax.ShapeDtypeStruct((2 * t, d), F32),
        compiler_params=_cparams(("arbitrary", "arbitrary")),
        name="moe_experts",
    )(tile_expert, n_active, n_valid, tok_of_slot, dst_of_slot, xn, wg, wu, wd)


def moe_dispatch_tables(info, *, tm, n_experts):
    t = info.shape[0]
    eidx = info[:, 2:4].astype(jnp.int32).reshape(-1)
    onehot = (eidx[:, None] == jnp.arange(n_experts, dtype=jnp.int32)[None, :]).astype(jnp.int32)
    csum = jnp.cumsum(onehot, axis=0)
    rank = jnp.sum((csum - onehot) * onehot, axis=1)
    counts = csum[-1]
    ntile_e = (counts + tm - 1) // tm
    tile_end = jnp.cumsum(ntile_e)
    tile_start = tile_end - ntile_e
    n_active = tile_end[-1]
    n_tiles = (2 * t) // tm + n_experts
    n_slots = n_tiles * tm
    slot = tile_start[eidx] * tm + rank
    pair = jnp.arange(2 * t, dtype=jnp.int32)
    tok_of_slot = jnp.zeros((n_slots,), jnp.int32).at[slot].set(pair // 2)
    dst_of_slot = jnp.zeros((n_slots,), jnp.int32).at[slot].set((pair % 2) * t + pair // 2)
    tiles = jnp.arange(n_tiles, dtype=jnp.int32)
    te = jnp.searchsorted(tile_end, jnp.minimum(tiles, n_active - 1), side="right").astype(jnp.int32)
    te = jnp.minimum(te, n_experts - 1)
    n_valid = jnp.clip(counts[te] - (tiles - tile_start[te]) * tm, 0, tm)
    n_valid = jnp.where(tiles < n_active, n_valid, 0).astype(jnp.int32)
    return (te, n_active.reshape(1).astype(jnp.int32), n_valid, tok_of_slot.reshape(n_tiles, 1, tm),
            dst_of_slot.reshape(n_tiles, 1, tm))


def _final_kernel(h_ref, y0_ref, y1_ref, info_ref, g_ref, o_ref):
    info = info_ref[...]
    x = h_ref[...] + info[:, 0:1] * y0_ref[...] + info[:, 1:2] * y1_ref[...]
    ms = jnp.mean(x * x, axis=-1, keepdims=True)
    o_ref[...] = x * lax.rsqrt(ms + EPS) * g_ref[...]


def final_combine(h, y, info, gain, *, tm=512):
    t, d = h.shape
    tm = min(tm, t)
    nt = t // tm
    return pl.pallas_call(
        _final_kernel,
        grid=(nt,),
        in_specs=[
            pl.BlockSpec((tm, d), lambda i: (i, 0)),
            pl.BlockSpec((tm, d), lambda i: (i, 0)),
            pl.BlockSpec((tm, d), lambda i: (nt + i, 0)),
            pl.BlockSpec((tm, LANES), lambda i: (i, 0)),
            pl.BlockSpec((1, d), lambda i: (0, 0)),
        ],
        out_specs=pl.BlockSpec((tm, d), lambda i: (i, 0)),
        out_shape=jax.ShapeDtypeStruct((t, d), F32),
        compiler_params=_cparams(("parallel",)),
        name="final_combine",
    )(h, y, y, info, gain.reshape(1, d).astype(F32))


def _softplus(x):
    return jnp.maximum(x, 0.0) + jnp.log1p(jnp.exp(-jnp.abs(x)))


def _gates_kernel(g_ref, p0_ref, p1_ref, col_ref, row_ref, *, mode, n_heads, rows):
    x = g_ref[...]
    lane = lax.broadcasted_iota(jnp.int32, x.shape, 1)
    if mode == "gdn":
        logdec = -jnp.exp(p0_ref[...]) * _softplus(x + p1_ref[...])
        second = jax.nn.sigmoid(x)
    else:
        logdec = -_softplus(-(x + p0_ref[...]))
        second = x + p1_ref[...]
    logdec = jnp.where(lane < 2 * n_heads, logdec, 0.0)
    ri = lax.broadcasted_iota(jnp.int32, (CHUNK, CHUNK), 0)
    ci = lax.broadcasted_iota(jnp.int32, (CHUNK, CHUNK), 1)
    lower = (ri >= ci).astype(F32)
    upper = (ri <= ci).astype(F32)
    lane_c = lax.broadcasted_iota(jnp.int32, (CHUNK, LANES), 1)
    fwd_lane = lane_c < n_heads
    for c in range(rows // CHUNK):
        sl = slice(c * CHUNK, (c + 1) * CHUNK)
        ld = logdec[sl]
        cum = jnp.where(fwd_lane, _dot_hi(lower, ld), _dot_hi(upper, ld))
        sec = second[sl]
        if mode == "mlstm":
            sec = pltpu.roll(cum, 2 * n_heads, axis=1) - sec
        out = jnp.where(lane_c < 2 * n_heads, cum, jnp.where(lane_c < 4 * n_heads, sec, 0.0))
        col_ref[sl, :] = out
        row_ref[c] = jnp.transpose(out)


def gate_prep(g, p0, p1, *, mode, n_heads, rows=512):
    t = g.shape[0]
    rows = min(rows, t)
    assert t % rows == 0 and rows % CHUNK == 0
    return pl.pallas_call(
        functools.partial(_gates_kernel, mode=mode, n_heads=n_heads, rows=rows),
        grid=(t // rows,),
        in_specs=[
            pl.BlockSpec((rows, LANES), lambda i: (i, 0)),
            pl.BlockSpec((1, LANES), lambda i: (0, 0)),
            pl.BlockSpec((1, LANES), lambda i: (0, 0)),
        ],
        out_specs=[
            pl.BlockSpec((rows, LANES), lambda i: (i, 0)),
            pl.BlockSpec((rows // CHUNK, LANES, CHUNK), lambda i: (i, 0, 0)),
        ],
        out_shape=[jax.ShapeDtypeStruct((t, LANES), F32), jax.ShapeDtypeStruct((t // CHUNK, LANES, CHUNK), F32)],
        compiler_params=_cparams(("parallel",)),
        name="gate_prep_" + mode,
    )(g, p0, p1)


def _lane_vec(*parts):
    flat = jnp.concatenate([p.reshape(-1).astype(F32) for p in parts])
    return jnp.zeros((1, LANES), F32).at[0, :flat.shape[0]].set(flat)


def _col(tile, lane, j):
    return jnp.sum(jnp.where(lane == j, tile, 0.0), axis=1, keepdims=True)


def _tri_masks(direction):
    ri = lax.broadcasted_iota(jnp.int32, (CHUNK, CHUNK), 0)
    ci = lax.broadcasted_iota(jnp.int32, (CHUNK, CHUNK), 1)
    if direction == 0:
        return ri >= ci, ri > ci
    return ri <= ci, ri < ci


def _unit_tri_inverse(m):
    ri = lax.broadcasted_iota(jnp.int32, m.shape, 0)
    ci = lax.broadcasted_iota(jnp.int32, m.shape, 1)
    p = jnp.where(ri == ci, 1.0, 0.0) - m
    mp = m
    for _ in range(5):
        mp = _dot_hi(mp, mp)
        p = p + _dot_hi(p, mp)
    return p


def _gdn_kernel(q_ref, k_ref, v_ref, z_ref, cwq_ref, cwk_ref, cwv_ref, gc_ref, gct_ref, gn_ref, o_ref,
                pad_ref, qn_ref, kn_ref, vn_ref, of_ref, ob_ref, *, seq, n_heads, conv_rows, dk):
    h = pl.program_id(1)
    n_chunks = seq // CHUNK
    half = GDN_CONV // 2

    for src, cw_ref, dst, kind in ((q_ref, cwq_ref, qn_ref, "q"), (k_ref, cwk_ref, kn_ref, "k"), (v_ref, cwv_ref, vn_ref, "v")):
        pad_ref[0:8, :] = jnp.zeros((8, LANES), F32)
        pad_ref[seq + 8:seq + 16, :] = jnp.zeros((8, LANES), F32)
        pad_ref[8:seq + 8, :] = src[0].astype(F32)
        w = cw_ref[...]

        def conv_block(i, c, w=w, dst=dst, kind=kind):
            t0 = pl.multiple_of(i * conv_rows, conv_rows)
            win = pad_ref[pl.ds(t0, conv_rows + 16), :]
            acc = jnp.zeros((conv_rows, LANES), F32)
            for j in range(GDN_CONV):
                off = 8 - half + j
                acc = acc + win[off:off + conv_rows, :] * w[j:j + 1, :]
            y = _silu(acc)
            if kind != "v":
                y = y * lax.rsqrt(jnp.sum(y * y, axis=-1, keepdims=True) + EPS)
            if kind == "q":
                y = y * (dk ** -0.5)
            dst[pl.ds(t0, conv_rows), :] = y
            return c

        lax.fori_loop(0, seq // conv_rows, conv_block, 0)

    lane = lax.broadcasted_iota(jnp.int32, (CHUNK, LANES), 1)

    def chunk(c, direction, state):
        r0 = pl.multiple_of(c * CHUNK, CHUNK)
        jd = direction * n_heads + h
        incl, strict = _tri_masks(direction)
        tile = gc_ref[0, pl.ds(r0, CHUNK), :]
        gcol = _col(tile, lane, jd)
        bcol = _col(tile, lane, 2 * n_heads + jd)
        grow = gct_ref[0, c, pl.ds(jd, 1), :]
        glast = gcol[CHUNK - 1:CHUNK] if direction == 0 else gcol[0:1]
        gam = jnp.exp(jnp.where(incl, gcol - grow, -jnp.inf))
        qc = qn_ref[pl.ds(r0, CHUNK), :]
        kc = kn_ref[pl.ds(r0, CHUNK), :]
        vc = vn_ref[pl.ds(r0, CHUNK), :]
        k16 = kc.astype(BF16)
        kk = _dot_nt(k16, k16)
        qk = _dot_nt(qc.astype(BF16), k16)
        tinv = _unit_tri_inverse(jnp.where(strict, kk * bcol * gam, 0.0))
        eg = jnp.exp(gcol)
        rhs = jnp.concatenate([vc * bcol, kc * (bcol * eg)], axis=1)
        sol = _dot_hi(tinv, rhs)
        u = sol[:, :LANES]
        w = sol[:, LANES:]
        s16 = state.astype(BF16)
        v_new = u - _dot(w.astype(BF16), s16)
        vn16 = v_new.astype(BF16)
        o = _dot((qc * eg).astype(BF16), s16) + _dot((qk * gam).astype(BF16), vn16)
        kd = (kc * jnp.exp(glast - gcol)).astype(BF16)
        state = state * jnp.exp(glast) + _dot_tn(kd, vn16)
        return r0, o, state

    def body(i, carry):
        sf, sb = carry
        r0, o, sf = chunk(i, 0, sf)
        of_ref[pl.ds(r0, CHUNK), :] = o
        r0, o, sb = chunk(n_chunks - 1 - i, 1, sb)
        ob_ref[pl.ds(r0, CHUNK), :] = o
        return sf, sb

    zero = jnp.zeros((LANES, LANES), F32)
    lax.fori_loop(0, n_chunks, body, (zero, zero))

    def epilogue(i, c):
        t0 = pl.multiple_of(i * conv_rows, conv_rows)
        o = of_ref[pl.ds(t0, conv_rows), :] + ob_ref[pl.ds(t0, conv_rows), :]
        ms = jnp.mean(o * o, axis=-1, keepdims=True)
        y = o * lax.rsqrt(ms + EPS) * gn_ref[...]
        z = z_ref[0, pl.ds(t0, conv_rows), :].astype(F32)
        o_ref[0, pl.ds(t0, conv_rows), :] = (y * _silu(z)).astype(o_ref.dtype)
        return c

    lax.fori_loop(0, seq // conv_rows, epilogue, 0)


def gdn_mixer(proj, conv_w, gcol, grow, gdn_norm, *, batch, seq, n_heads, col0, conv_rows=256):
    hb = n_heads
    conv_rows = min(conv_rows, seq)
    n_chunks = seq // CHUNK

    def head_spec(group):
        return pl.BlockSpec((1, seq, LANES), lambda b, h, group=group: (b, 0, col0 + group * hb + h))

    def conv_spec(group):
        return pl.BlockSpec((GDN_CONV, LANES), lambda b, h, group=group: (0, group * hb + h))

    return pl.pallas_call(
        functools.partial(_gdn_kernel, seq=seq, n_heads=n_heads, conv_rows=conv_rows, dk=LANES),
        grid=(batch, n_heads),
        in_specs=[
            head_spec(0), head_spec(1), head_spec(2), head_spec(3),
            conv_spec(0), conv_spec(1), conv_spec(2),
            pl.BlockSpec((1, seq, LANES), lambda b, h: (b, 0, 0)),
            pl.BlockSpec((1, n_chunks, LANES, CHUNK), lambda b, h: (b, 0, 0, 0)),
            pl.BlockSpec((1, LANES), lambda b, h: (0, 0)),
        ],
        out_specs=pl.BlockSpec((1, seq, LANES), lambda b, h: (b, 0, h)),
        out_shape=jax.ShapeDtypeStruct((batch, seq, n_heads * LANES), BF16),
        scratch_shapes=[
            pltpu.VMEM((seq + 16, LANES), F32),
            pltpu.VMEM((seq, LANES), F32),
            pltpu.VMEM((seq, LANES), F32),
            pltpu.VMEM((seq, LANES), F32),
            pltpu.VMEM((seq, LANES), F32),
            pltpu.VMEM((seq, LANES), F32),
        ],
        compiler_params=_cparams(("parallel", "arbitrary")),
        name="gdn_mixer",
    )(proj, proj, proj, proj, conv_w, conv_w, conv_w,
      gcol.reshape(batch, seq, LANES), grow.reshape(batch, n_chunks, LANES, CHUNK),
      gdn_norm.reshape(1, LANES).astype(F32))


def _mlstm_kernel(q_ref, k_ref, v_ref, og_ref, gc_ref, gct_ref, nrm_ref, o_ref,
                  hf_ref, hb_ref, st_ref, *, seq, n_heads, rows, dqk, dv):
    p = pl.program_id(1)
    n_chunks = seq // CHUNK
    lane = lax.broadcasted_iota(jnp.int32, (CHUNK, LANES), 1)
    ones = jnp.ones((CHUNK, dv), BF16)
    scale = dqk ** -0.5

    def chunk(c, direction, hh, m):
        r0 = pl.multiple_of(c * CHUNK, CHUNK)
        jd = direction * n_heads + 2 * p + hh
        incl, _ = _tri_masks(direction)
        tile = gc_ref[0, pl.ds(r0, CHUNK), :]
        bc = _col(tile, lane, jd)
        ac = _col(tile, lane, 2 * n_heads + jd)
        arow = gct_ref[0, c, pl.ds(2 * n_heads + jd, 1), :]
        blast = bc[CHUNK - 1:CHUNK] if direction == 0 else bc[0:1]
        dmat = jnp.where(incl, bc - arow, NEG)
        dmax = jnp.max(dmat, axis=1, keepdims=True)
        q = q_ref[0, pl.ds(r0, CHUNK), hh * dqk:(hh + 1) * dqk]
        k = k_ref[0, pl.ds(r0, CHUNK), hh * dqk:(hh + 1) * dqk]
        vaug = jnp.concatenate([v_ref[0, pl.ds(r0, CHUNK), hh * dv:(hh + 1) * dv], ones], axis=1)
        qk = _dot_nt(q, k) * scale
        inter = bc + m
        mj = jnp.maximum(inter, dmax)
        s = qk * jnp.exp(dmat - mj)
        iw = jnp.exp(inter - mj)
        idx = direction * 2 + hh
        state = st_ref[idx]
        num = iw * _dot(q, state.astype(BF16)) + _dot(s.astype(BF16), vaug)
        den = num[:, dv:dv + 1]
        hout = num[:, :dv] / jnp.maximum(jnp.abs(den), jnp.exp(-mj))
        wl = blast - ac
        m_new = jnp.maximum(blast + m, jnp.max(wl, axis=0, keepdims=True))
        ws = jnp.exp(wl - m_new)
        decay = jnp.exp(blast + m - m_new)
        kw = (k.astype(F32) * (ws * scale)).astype(BF16)
        st_ref[idx] = decay * state + _dot_tn(kw, vaug)
        return r0, hout, m_new

    st_ref[...] = jnp.zeros_like(st_ref)

    def body(i, ms):
        out = []
        for direction in range(2):
            c = i if direction == 0 else n_chunks - 1 - i
            dst = hf_ref if direction == 0 else hb_ref
            for hh in range(2):
                r0, hout, m_new = chunk(c, direction, hh, ms[direction * 2 + hh])
                dst[pl.ds(r0, CHUNK), hh * dv:(hh + 1) * dv] = hout
                out.append(m_new)
        return tuple(out)

    zero = jnp.zeros((1, 1), F32)
    lax.fori_loop(0, n_chunks, body, (zero, zero, zero, zero))

    def epilogue(i, c):
        t0 = pl.multiple_of(i * rows, rows)
        g = nrm_ref[0]
        og = og_ref[0, pl.ds(t0, rows), :].astype(F32)
        for hh in range(2):
            sl = slice(hh * dv, (hh + 1) * dv)
            x = hf_ref[pl.ds(t0, rows), sl] + hb_ref[pl.ds(t0, rows), sl]
            ms = jnp.mean(x * x, axis=-1, keepdims=True)
            y = x * lax.rsqrt(ms + EPS) * g[:, sl]
            o_ref[0, pl.ds(t0, rows), sl] = (y * jax.nn.sigmoid(og[:, sl])).astype(o_ref.dtype)
        return c

    lax.fori_loop(0, seq // rows, epilogue, 0)


def mlstm_mixer(proj, gcol, grow, mlstm_norm, *, batch, seq, n_heads, q_blk, k_blk, v_blk, o_blk, rows=256):
    dqk, dv = MLSTM_DQK, MLSTM_DV
    n_chunks = seq // CHUNK
    rows = min(rows, seq)
    npair = n_heads // 2
    return pl.pallas_call(
        functools.partial(_mlstm_kernel, seq=seq, n_heads=n_heads, rows=rows, dqk=dqk, dv=dv),
        grid=(batch, npair),
        in_specs=[
            pl.BlockSpec((1, seq, 2 * dqk), lambda b, p: (b, 0, q_blk + p)),
            pl.BlockSpec((1, seq, 2 * dqk), lambda b, p: (b, 0, k_blk + p)),
            pl.BlockSpec((1, seq, 2 * dv), lambda b, p: (b, 0, v_blk + p)),
            pl.BlockSpec((1, seq, 2 * dv), lambda b, p: (b, 0, o_blk + p)),
            pl.BlockSpec((1, seq, LANES), lambda b, p: (b, 0, 0)),
            pl.BlockSpec((1, n_chunks, LANES, CHUNK), lambda b, p: (b, 0, 0, 0)),
            pl.BlockSpec((1, 1, 2 * dv), lambda b, p: (p, 0, 0)),
        ],
        out_specs=pl.BlockSpec((1, seq, 2 * dv), lambda b, p: (b, 0, p)),
        out_shape=jax.ShapeDtypeStruct((batch, seq, n_heads * dv), BF16),
        scratch_shapes=[
            pltpu.VMEM((seq, 2 * dv), F32),
            pltpu.VMEM((seq, 2 * dv), F32),
            pltpu.VMEM((4, dqk, 2 * dv), F32),
        ],
        compiler_params=_cparams(("parallel", "arbitrary")),
        name="mlstm_mixer",
    )(proj, proj, proj, proj, gcol.reshape(batch, seq, LANES), grow.reshape(batch, n_chunks, LANES, CHUNK),
      mlstm_norm.reshape(npair, 1, 2 * dv).astype(F32))


def _dsw_kernel(slope_ref, q_ref, k_ref, v_ref, o_ref, l_ref, *, u_len, dil, half, n_heads, qb):
    h = pl.program_id(1) % n_heads
    bias_step = slope_ref[h] * float(dil)
    kb = qb + 2 * half
    scale = LANES ** -0.5

    def block(i, c):
        u0 = pl.multiple_of(i * qb, qb)
        ks = jnp.clip(u0 - half, 0, u_len - kb)
        ks = pl.multiple_of(ks, half)
        q = q_ref[0, pl.ds(u0, qb), :]
        k = k_ref[0, pl.ds(ks, kb), :]
        v = v_ref[0, pl.ds(ks, kb), :]
        s = _dot_nt(q, k) * scale
        qpos = u0 + lax.broadcasted_iota(jnp.int32, (qb, kb), 0)
        kpos = ks + lax.broadcasted_iota(jnp.int32, (qb, kb), 1)
        dist = jnp.abs(kpos - qpos)
        s = s - bias_step * dist.astype(F32)
        s = jnp.where(dist <= half, s, NEG)
        mx = jnp.max(s, axis=-1, keepdims=True)
        p = jnp.exp(s - mx)
        den = jnp.sum(p, axis=-1, keepdims=True)
        o = _dot(p.astype(BF16), v) / den
        o_ref[0, pl.ds(u0, qb), :] = o
        l_ref[0, pl.ds(u0, qb), :] = jnp.broadcast_to(mx + jnp.log(den), (qb, LANES))
        return c

    lax.fori_loop(0, u_len // qb, block, 0)


def dsw_branch(proj, slopes, *, batch, seq, width, n_heads, q_blk, k_blk, v_blk, window, dil, qb=128):
    u_len = seq // dil
    half = window // (2 * dil)
    wb = width // LANES
    view = proj.reshape(batch, u_len, dil * width)

    def spec(blk):
        return pl.BlockSpec((1, u_len, LANES), lambda b, j, s, blk=blk: (b, 0, (j // n_heads) * wb + blk + j % n_heads))

    out_spec = pl.BlockSpec((1, u_len, LANES), lambda b, j, s: (b, 0, j))
    shape = jax.ShapeDtypeStruct((batch, u_len, dil * n_heads * LANES), F32)
    o, l = pl.pallas_call(
        functools.partial(_dsw_kernel, u_len=u_len, dil=dil, half=half, n_heads=n_heads, qb=qb),
        grid_spec=pltpu.PrefetchScalarGridSpec(
            num_scalar_prefetch=1,
            grid=(batch, dil * n_heads),
            in_specs=[spec(q_blk), spec(k_blk), spec(v_blk)],
            out_specs=[out_spec, out_spec],
        ),
        out_shape=[shape, shape],
        compiler_params=_cparams(("parallel", "arbitrary")),
        name="dsw_branch_r%d" % dil,
    )(slopes, view, view, view)
    return o.reshape(batch * seq, n_heads * LANES), l.reshape(batch * seq, n_heads * LANES)


def _dsw_merge_kernel(o1, l1, o2, l2, o3, l3, out_ref):
    a, b, c = l1[...], l2[...], l3[...]
    mx = jnp.maximum(jnp.maximum(a, b), c)
    ea, eb, ec = jnp.exp(a - mx), jnp.exp(b - mx), jnp.exp(c - mx)
    tot = ea + eb + ec
    out_ref[...] = ((o1[...] * ea + o2[...] * eb + o3[...] * ec) / tot).astype(out_ref.dtype)


def dsw_merge(parts, *, tm=512):
    t, w = parts[0][0].shape
    tm = min(tm, t)
    flat = [a for pair in parts for a in pair]
    spec = pl.BlockSpec((tm, w), lambda i: (i, 0))
    return pl.pallas_call(
        _dsw_merge_kernel,
        grid=(t // tm,),
        in_specs=[spec] * 6,
        out_specs=spec,
        out_shape=jax.ShapeDtypeStruct((t, w), BF16),
        compiler_params=_cparams(("parallel",)),
        name="dsw_merge",
    )(*flat)


def _mla_kernel(q_ref, tq_ref, kn_ref, v_ref, kr_ref, krr_ref, ck_ref, sk_ref, o_ref, k_scr):
    @pl.when(pl.program_id(2) == 0)
    def _():
        k_scr[:, :LANES] = kn_ref[0]
        rk = kr_ref[0].astype(F32) * ck_ref[...] + krr_ref[0].astype(F32) * sk_ref[...]
        k_scr[:, LANES:] = rk.astype(BF16)

    q = (q_ref[0].astype(F32) * tq_ref[...]).astype(BF16)
    s = _dot_nt(q, k_scr[...])
    mx = jnp.max(s, axis=-1, keepdims=True)
    p = jnp.exp(s - mx)
    den = jnp.sum(p, axis=-1, keepdims=True)
    o = _dot(p.astype(BF16), v_ref[0]) / den
    o_ref[0] = o.astype(o_ref.dtype)


def mla_attention(q_up, kv_up, proj, tq_tab, ck_tab, sk_tab, *, batch, seq, n_heads, kr_blk, krr_blk, tq=512):
    tq = min(tq, seq)
    return pl.pallas_call(
        _mla_kernel,
        grid=(batch, n_heads, seq // tq),
        in_specs=[
            pl.BlockSpec((1, tq, 2 * LANES), lambda b, h, i: (b, i, h)),
            pl.BlockSpec((tq, 2 * LANES), lambda b, h, i: (i, 0)),
            pl.BlockSpec((1, seq, LANES), lambda b, h, i: (b, 0, 2 * h)),
            pl.BlockSpec((1, seq, LANES), lambda b, h, i: (b, 0, 2 * h + 1)),
            pl.BlockSpec((1, seq, LANES), lambda b, h, i: (b, 0, kr_blk)),
            pl.BlockSpec((1, seq, LANES), lambda b, h, i: (b, 0, krr_blk)),
            pl.BlockSpec((seq, LANES), lambda b, h, i: (0, 0)),
            pl.BlockSpec((seq, LANES), lambda b, h, i: (0, 0)),
        ],
        out_specs=pl.BlockSpec((1, tq, LANES), lambda b, h, i: (b, i, h)),
        out_shape=jax.ShapeDtypeStruct((batch, seq, n_heads * LANES), BF16),
        scratch_shapes=[pltpu.VMEM((seq, 2 * LANES), BF16)],
        compiler_params=_cparams(("parallel", "parallel", "arbitrary")),
        name="mla_attention",
    )(q_up, tq_tab, kv_up, kv_up, proj, proj, ck_tab, sk_tab)


def _rot_cols(w):
    half = w.shape[-1] // 2
    return jnp.concatenate([-w[..., half:], w[..., :half]], axis=-1)


def layer_even(h, batch, seq, mix_norm, w_in, conv_w, a_log, dt_bias, gdn_norm, w_out, ffn_norm, w_gate, w_up, w_down,
               *, tm=512):
    d = h.shape[1]
    nh = GDN_HEADS
    qk = nh * LANES
    gate0 = 4 * qk
    att0 = gate0 + 4 * nh
    w_main = jnp.concatenate([w_in[:, :gate0], w_in[:, att0:]], axis=1).astype(BF16)
    w_gates = jnp.zeros((d, LANES), F32).at[:, :4 * nh].set(w_in[:, gate0:att0]).astype(BF16)
    proj = norm_matmul(h, mix_norm, w_main, out_dtype=BF16, tm=tm, name="l0_in_proj")
    graw = norm_matmul(h, mix_norm, w_gates, out_dtype=F32, tm=tm, name="l0_gate_proj")
    gcol, grow = gate_prep(graw, _lane_vec(a_log), _lane_vec(dt_bias), mode="gdn", n_heads=nh)
    width = proj.shape[1]
    proj3 = proj.reshape(batch, seq, width)
    o_a = gdn_mixer(proj3, conv_w.astype(F32), gcol, grow, gdn_norm, batch=batch, seq=seq, n_heads=nh, col0=0)
    slopes = (2.0 ** (-8.0 * jnp.arange(1, DSW_HEADS + 1, dtype=F32) / DSW_HEADS)).astype(F32)
    parts = [dsw_branch(proj3, slopes, batch=batch, seq=seq, width=width, n_heads=DSW_HEADS,
                        q_blk=4 * nh, k_blk=4 * nh + DSW_HEADS, v_blk=4 * nh + 2 * DSW_HEADS, window=win, dil=r)
             for win, r in DSW_PAIRS]
    o_b = dsw_merge(parts, tm=tm)
    na = nh * LANES
    h = proj_residual(o_a.reshape(batch * seq, na), o_b, w_out[:na].astype(BF16), w_out[na:].astype(BF16), h,
                      tm=tm, name="l0_out_proj")
    return ffn_residual(h, ffn_norm, w_gate.astype(BF16), w_up.astype(BF16), w_down.astype(BF16), tm=tm)


def layer_odd_mixers(h, batch, seq, mix_norm, w_in, q_norm, kv_norm, w_uq, w_ukv, ig_bias, fg_bias, mlstm_norm, w_out,
                     *, tm=512):
    d = h.shape[1]
    nh = MLSTM_HEADS
    sizes = (MLA_Q_RANK, MLA_KV_RANK, MLA_ROPE, nh * MLSTM_DQK, nh * MLSTM_DQK, nh * MLSTM_DV, nh * MLSTM_DV, 2 * nh, 2 * nh)
    cuts = np.cumsum((0,) + sizes)
    w_cq, w_ckv, w_kr, w_mq, w_mk, w_mv, w_mo, w_mi, w_mf = (w_in[:, cuts[i]:cuts[i + 1]] for i in range(9))
    w_krot = _rot_cols(w_kr)
    w_main = jnp.concatenate([w_cq, w_kr, w_kr, w_krot, w_krot, w_ckv, w_mq, w_mk, w_mv, w_mo], axis=1).astype(BF16)
    w_gates = jnp.zeros((d, LANES), F32).at[:, :2 * nh].set(w_mf).at[:, 2 * nh:4 * nh].set(w_mi).astype(BF16)
    proj = norm_matmul(h, mix_norm, w_main, out_dtype=BF16, tm=tm, name="l1_in_proj")
    graw = norm_matmul(h, mix_norm, w_gates, out_dtype=F32, tm=tm, name="l1_gate_proj")
    gcol, grow = gate_prep(graw, _lane_vec(fg_bias), _lane_vec(jnp.zeros((2 * nh,), F32), ig_bias), mode="mlstm", n_heads=nh)
    width = proj.shape[1]
    proj3 = proj.reshape(batch, seq, width)

    hq = MLA_HEADS
    wq = w_uq.reshape(MLA_Q_RANK, hq, MLA_NOPE + MLA_ROPE)
    wq_rope = wq[:, :, MLA_NOPE:]
    wq_all = jnp.concatenate([wq[:, :, :MLA_NOPE], wq_rope, _rot_cols(wq_rope)], axis=-1).reshape(MLA_Q_RANK, hq * 2 * LANES)
    q_up = norm_matmul(proj, q_norm, wq_all.astype(BF16), out_dtype=BF16, k_block=0, k_width=MLA_Q_RANK, tm=tm, name="mla_q_up")
    kv_blk = (MLA_Q_RANK + 2 * LANES) // MLA_KV_RANK
    kv_up = norm_matmul(proj, kv_norm, w_ukv.astype(BF16), out_dtype=BF16, k_block=kv_blk, k_width=MLA_KV_RANK, tm=tm, name="mla_kv_up")
    pos = jnp.arange(seq, dtype=F32)
    freqs = ROPE_THETA ** (-jnp.arange(0, MLA_ROPE, 2, dtype=F32) / MLA_ROPE)
    ang = pos[:, None] * freqs[None, :]
    cos, sin = jnp.cos(ang), jnp.sin(ang)
    scale = (MLA_NOPE + MLA_ROPE) ** -0.5
    tq_tab = scale * jnp.concatenate([jnp.ones((seq, MLA_NOPE), F32), cos, cos, sin, sin], axis=1)
    ck_tab = jnp.concatenate([cos] * 4, axis=1)
    sk_tab = jnp.concatenate([sin] * 4, axis=1)
    kr_blk = MLA_Q_RANK // LANES
    o_c = mla_attention(q_up.reshape(batch, seq, -1), kv_up.reshape(batch, seq, -1), proj3, tq_tab, ck_tab, sk_tab,
                        batch=batch, seq=seq, n_heads=hq, kr_blk=kr_blk, krr_blk=kr_blk + 1)

    mq0 = kr_blk + 2 + MLA_KV_RANK // LANES
    mk0 = mq0 + nh * MLSTM_DQK // LANES
    mv0 = mk0 + nh * MLSTM_DQK // LANES
    mo0 = mv0 + nh * MLSTM_DV // LANES
    o_d = mlstm_mixer(proj3, gcol, grow, mlstm_norm, batch=batch, seq=seq, n_heads=nh,
                      q_blk=mq0, k_blk=mk0, v_blk=mv0 // 2, o_blk=mo0 // 2)
    nc = hq * MLA_V
    return proj_residual(o_c.reshape(batch * seq, nc), o_d.reshape(batch * seq, -1), w_out[:nc].astype(BF16),
                         w_out[nc:].astype(BF16), h, tm=tm, name="l1_out_proj")


def moe_block(h, ffn_norm, w_router, b_router, we_gate, we_up, we_down, final_norm, *, tm=512, tf=512, tm_tok=512):
    t = h.shape[0]
    n_experts = we_gate.shape[0]
    tm = min(tm, t)
    xn, info = router(h, ffn_norm, w_router, b_router, tm=tm_tok)
    te, n_active, n_valid, tok_of_slot, dst_of_slot = moe_dispatch_tables(info, tm=tm, n_experts=n_experts)
    y = moe_experts(xn, te, n_active, n_valid, tok_of_slot, dst_of_slot, we_gate.astype(BF16), we_up.astype(BF16),
                    we_down.astype(BF16), tm=tm, tf=tf)
    return final_combine(h, y, info, final_norm, tm=tm_tok)


def kernel(x, even_mix_norm, even_w_in, even_conv_w, even_a_log, even_dt_bias, even_gdn_norm, even_w_out,
           even_ffn_norm, even_w_gate, even_w_up, even_w_down, odd_mix_norm, odd_w_in, odd_q_norm, odd_kv_norm,
           odd_w_uq, odd_w_ukv, odd_ig_bias, odd_fg_bias, odd_mlstm_norm, odd_w_out, odd_ffn_norm,
           odd_w_router, odd_b_router, odd_we_gate, odd_we_up, odd_we_down, final_norm):
    batch, seq, d = x.shape
    h = x.reshape(batch * seq, d)
    h = layer_even(h, batch, seq, even_mix_norm[0], even_w_in[0], even_conv_w[0], even_a_log[0], even_dt_bias[0],
                   even_gdn_norm[0], even_w_out[0], even_ffn_norm[0], even_w_gate[0], even_w_up[0], even_w_down[0])
    h = layer_odd_mixers(h, batch, seq, odd_mix_norm[0], odd_w_in[0], odd_q_norm[0], odd_kv_norm[0], odd_w_uq[0],
                         odd_w_ukv[0], odd_ig_bias[0], odd_fg_bias[0], odd_mlstm_norm[0], odd_w_out[0])
    out = moe_block(h, odd_ffn_norm[0], odd_w_router[0], odd_b_router[0], odd_we_gate[0], odd_we_up[0],
                    odd_we_down[0], final_norm)
    return out.reshape(batch, seq, d)
```

```python
import functools
import math

import jax
import jax.numpy as jnp
import numpy as np
from jax import lax
from jax.experimental import pallas as pl
from jax.experimental.pallas import tpu as pltpu

F32 = jnp.float32
BF16 = jnp.bfloat16
EPS = 1e-6
NEG = -1e30
HI = lax.Precision.HIGHEST
LOG2E = math.log2(math.e)

LANES = 128
CHUNK = 64
SUB = 16
VMEM_LIMIT = 56 * 1024 * 1024

GDN_HEADS = 8
GDN_CONV = 5
DSW_HEADS = 8
DSW_PAIRS = ((128, 1), (512, 4), (2048, 16))
MLA_HEADS = 8
MLA_Q_RANK = 768
MLA_KV_RANK = 512
MLA_NOPE = 128
MLA_ROPE = 64
MLA_V = 128
ROPE_THETA = 10000.0
MLSTM_HEADS = 8
MLSTM_DQK = 64
MLSTM_DV = 128


def _cparams(sem):
    return pltpu.CompilerParams(dimension_semantics=sem, vmem_limit_bytes=VMEM_LIMIT)


def _dot(a, b):
    return jnp.dot(a, b, preferred_element_type=F32)


def _dot_nt(a, b):
    return lax.dot_general(a, b, (((1,), (1,)), ((), ())), preferred_element_type=F32)


def _dot_tn(a, b):
    return lax.dot_general(a, b, (((0,), (0,)), ((), ())), preferred_element_type=F32)


def _dot_hi(a, b):
    return jnp.dot(a, b, preferred_element_type=F32, precision=HI)


def _mm16(a, b):
    return _dot(a.astype(BF16), b.astype(BF16))


def _nm_kernel(x_ref, g_ref, w_ref, o_ref, xn_ref, *, use_norm):
    @pl.when(pl.program_id(1) == 0)
    def _():
        x = x_ref[...].astype(F32)
        if use_norm:
            ms = jnp.mean(x * x, axis=-1, keepdims=True)
            x = x * lax.rsqrt(ms + EPS) * g_ref[...]
        xn_ref[...] = x.astype(BF16)

    o_ref[...] = _dot(xn_ref[...], w_ref[...]).astype(o_ref.dtype)


def norm_matmul(x, gain, w, *, out_dtype, k_block=0, k_width=None, tm=512, tn=512, use_norm=True, name="norm_matmul"):
    t = x.shape[0]
    kw = x.shape[1] if k_width is None else k_width
    n = w.shape[1]
    tm = min(tm, t)
    tn = min(tn, n)
    assert t % tm == 0 and n % tn == 0 and w.shape[0] == kw
    g2 = gain.reshape(1, kw).astype(F32)
    return pl.pallas_call(
        functools.partial(_nm_kernel, use_norm=use_norm),
        grid=(t // tm, n // tn),
        in_specs=[
            pl.BlockSpec((tm, kw), lambda i, j: (i, k_block)),
            pl.BlockSpec((1, kw), lambda i, j: (0, 0)),
            pl.BlockSpec((kw, tn), lambda i, j: (0, j)),
        ],
        out_specs=pl.BlockSpec((tm, tn), lambda i, j: (i, j)),
        out_shape=jax.ShapeDtypeStruct((t, n), out_dtype),
        scratch_shapes=[pltpu.VMEM((tm, kw), BF16)],
        compiler_params=_cparams(("parallel", "arbitrary")),
        name=name,
    )(x, g2, w)


def _proj_res_kernel(a_ref, b_ref, w1_ref, w2_ref, r_ref, o_ref):
    acc = _dot(a_ref[...], w1_ref[...]) + _dot(b_ref[...], w2_ref[...])
    o_ref[...] = r_ref[...] + acc


def proj_residual(a, b, w1, w2, res, *, tm=512, name="proj_residual"):
    t, ka = a.shape
    kb = b.shape[1]
    n = w1.shape[1]
    tm = min(tm, t)
    assert t % tm == 0
    return pl.pallas_call(
        _proj_res_kernel,
        grid=(t // tm,),
        in_specs=[
            pl.BlockSpec((tm, ka), lambda i: (i, 0)),
            pl.BlockSpec((tm, kb), lambda i: (i, 0)),
            pl.BlockSpec((ka, n), lambda i: (0, 0)),
            pl.BlockSpec((kb, n), lambda i: (0, 0)),
            pl.BlockSpec((tm, n), lambda i: (i, 0)),
        ],
        out_specs=pl.BlockSpec((tm, n), lambda i: (i, 0)),
        out_shape=jax.ShapeDtypeStruct((t, n), F32),
        compiler_params=_cparams(("parallel",)),
        name=name,
    )(a, b, w1, w2, res)


def _silu(a):
    return a * jax.nn.sigmoid(a)


def _ffn_kernel(h_ref, g_ref, wg_ref, wu_ref, wd_ref, o_ref, xn_ref, acc_ref):
    j = pl.program_id(1)

    @pl.when(j == 0)
    def _():
        x = h_ref[...]
        ms = jnp.mean(x * x, axis=-1, keepdims=True)
        xn_ref[...] = (x * lax.rsqrt(ms + EPS) * g_ref[...]).astype(BF16)
        acc_ref[...] = jnp.zeros_like(acc_ref)

    xn = xn_ref[...]
    mid = (_silu(_dot(xn, wg_ref[...])) * _dot(xn, wu_ref[...])).astype(BF16)
    acc_ref[...] += _dot(mid, wd_ref[...])

    @pl.when(j == pl.num_programs(1) - 1)
    def _():
        o_ref[...] = h_ref[...] + acc_ref[...]


def ffn_residual(h, gain, wg, wu, wd, *, tm=512, tf=512, name="ffn_swiglu"):
    t, d = h.shape
    f = wg.shape[1]
    tm = min(tm, t)
    tf = min(tf, f)
    assert t % tm == 0 and f % tf == 0
    return pl.pallas_call(
        _ffn_kernel,
        grid=(t // tm, f // tf),
        in_specs=[
            pl.BlockSpec((tm, d), lambda i, j: (i, 0)),
            pl.BlockSpec((1, d), lambda i, j: (0, 0)),
            pl.BlockSpec((d, tf), lambda i, j: (0, j)),
            pl.BlockSpec((d, tf), lambda i, j: (0, j)),
            pl.BlockSpec((tf, d), lambda i, j: (j, 0)),
        ],
        out_specs=pl.BlockSpec((tm, d), lambda i, j: (i, 0)),
        out_shape=jax.ShapeDtypeStruct((t, d), F32),
        scratch_shapes=[pltpu.VMEM((tm, d), BF16), pltpu.VMEM((tm, d), F32)],
        compiler_params=_cparams(("parallel", "arbitrary")),
        name=name,
    )(h, gain.reshape(1, d).astype(F32), wg, wu, wd)


def _router_kernel(h_ref, g_ref, wr_ref, br_ref, xn_ref, info_ref, *, n_experts):
    x = h_ref[...]
    ms = jnp.mean(x * x, axis=-1, keepdims=True)
    xn = x * lax.rsqrt(ms + EPS) * g_ref[...]
    xn_ref[...] = xn
    logits = _dot_hi(xn, wr_ref[...]) + br_ref[...]
    lane = lax.broadcasted_iota(jnp.int32, logits.shape, 1)
    real = lane < n_experts
    lg = jnp.where(real, logits, -jnp.inf)
    mx = jnp.max(lg, axis=-1, keepdims=True)
    ex = jnp.exp(lg - mx)
    probs = ex / jnp.sum(ex, axis=-1, keepdims=True)
    probs = jnp.where(real, probs, -1.0)
    v1 = jnp.max(probs, axis=-1, keepdims=True)
    i1 = jnp.min(jnp.where(probs == v1, lane, LANES), axis=-1, keepdims=True)
    rest = jnp.where(lane == i1, -1.0, probs)
    v2 = jnp.max(rest, axis=-1, keepdims=True)
    i2 = jnp.min(jnp.where(rest == v2, lane, LANES), axis=-1, keepdims=True)
    tot = v1 + v2
    info = jnp.where(lane == 0, v1 / tot, 0.0)
    info = jnp.where(lane == 1, v2 / tot, info)
    info = jnp.where(lane == 2, i1.astype(F32), info)
    info = jnp.where(lane == 3, i2.astype(F32), info)
    info_ref[...] = info


def router(h, gain, w_router, b_router, *, tm=512):
    t, d = h.shape
    e = w_router.shape[1]
    tm = min(tm, t)
    wr = jnp.zeros((d, LANES), F32).at[:, :e].set(w_router.astype(F32))
    br = jnp.zeros((1, LANES), F32).at[0, :e].set(b_router.astype(F32))
    return pl.pallas_call(
        functools.partial(_router_kernel, n_experts=e),
        grid=(t // tm,),
        in_specs=[
            pl.BlockSpec((tm, d), lambda i: (i, 0)),
            pl.BlockSpec((1, d), lambda i: (0, 0)),
            pl.BlockSpec((d, LANES), lambda i: (0, 0)),
            pl.BlockSpec((1, LANES), lambda i: (0, 0)),
        ],
        out_specs=[pl.BlockSpec((tm, d), lambda i: (i, 0)), pl.BlockSpec((tm, LANES), lambda i: (i, 0))],
        out_shape=[jax.ShapeDtypeStruct((t, d), F32), jax.ShapeDtypeStruct((t, LANES), F32)],
        compiler_params=_cparams(("parallel",)),
        name="router",
    )(h, gain.reshape(1, d).astype(F32), wr, br)


def _moe_kernel(te_ref, na_ref, nv_ref, tok_ref, tokn_ref, dst_ref, dstp_ref, xn_hbm, wg_ref, wu_ref, wd_ref, out_hbm,
                xrow_ref, xb_ref, acc_ref, sem_in, sem_out, *, tm, unroll, rows_per_step):
    i = pl.program_id(0)
    j = pl.program_id(1)
    nj = pl.num_programs(1)
    n_active = na_ref[0]
    active = i < n_active
    slot = i % 2
    n_valid = nv_ref[i]
    n_valid_prev = nv_ref[jnp.maximum(i - 1, 0)]

    def gather_copy(ids_ref, r, s):
        return pltpu.make_async_copy(xn_hbm.at[pl.ds(ids_ref[0, 0, r], 1)], xrow_ref.at[s, pl.ds(r, 1)], sem_in.at[s])

    def scatter_copy(ids_ref, r, s):
        return pltpu.make_async_copy(acc_ref.at[s, pl.ds(r, 1)], out_hbm.at[pl.ds(ids_ref[0, 0, r], 1)], sem_out.at[s])

    def for_rows(n, fn, unroll=1):
        def body(r, c):
            fn(r)
            return c

        if isinstance(n, int):
            lax.fori_loop(0, n, body, 0, unroll=unroll)
            return

        def body_blk(b, c):
            for u in range(unroll):
                fn(b * unroll + u)
            return c

        n_blk = n // unroll
        lax.fori_loop(0, n_blk, body_blk, 0)
        lax.fori_loop(n_blk * unroll, n, body, 0)

    @pl.when(active & (j == 0))
    def _():
        @pl.when(i == 0)
        def _():
            for_rows(tm, lambda r: gather_copy(tok_ref, r, 0).start(), unroll)

        for_rows(tm, lambda r: gather_copy(tok_ref, r, slot).wait(), unroll)
        xb_ref[...] = xrow_ref[slot].astype(BF16)

    @pl.when(active)
    def _():
        for u in range(rows_per_step):
            gather_copy(tokn_ref, j * rows_per_step + u, 1 - slot).start()
        xb = xb_ref[...]
        mid = (_silu(_dot(xb, wg_ref[0])) * _dot(xb, wu_ref[0])).astype(BF16)
        part = _dot(mid, wd_ref[0])

        @pl.when(j == 0)
        def _():
            acc_ref[slot] = part

        @pl.when(j > 0)
        def _():
            acc_ref[slot] += part

    @pl.when(active & (j == nj - 1))
    def _():
        @pl.when(i > 0)
        def _():
            for_rows(n_valid_prev, lambda r: scatter_copy(dstp_ref, r, 1 - slot).wait(), unroll)

        for_rows(n_valid, lambda r: scatter_copy(dst_ref, r, slot).start(), unroll)

        @pl.when(i == n_active - 1)
        def _():
            for_rows(n_valid, lambda r: scatter_copy(dst_ref, r, slot).wait(), unroll)
            for_rows(tm, lambda r: gather_copy(tokn_ref, r, 1 - slot).wait(), unroll)


def moe_experts(xn, tile_expert, n_active, n_valid, tok_of_slot, dst_of_slot, wg, wu, wd, *, tm, tf):
    t, d = xn.shape
    e, _, f = wg.shape
    n_tiles = tok_of_slot.shape[0]
    tf = min(tf, f)
    nj = f // tf
    assert tm % nj == 0, "the row prefetch is spread evenly over the grid steps of a tile"

    def w_in_map(i, j, te, na, nv):
        return (te[i], 0, jnp.where(i < na[0], j, nj - 1))

    def w_dn_map(i, j, te, na, nv):
        return (te[i], jnp.where(i < na[0], j, nj - 1), 0)

    def ids_spec(shift):
        return pl.BlockSpec((1, 1, tm), lambda i, j, te, na, nv: (jnp.clip(i + shift, 0, n_tiles - 1), 0, 0),
                            memory_space=pltpu.SMEM)

    grid_spec = pltpu.PrefetchScalarGridSpec(
        num_scalar_prefetch=3,
        grid=(n_tiles, nj),
        in_specs=[
            ids_spec(0), ids_spec(1), ids_spec(0), ids_spec(-1),
            pl.BlockSpec(memory_space=pl.ANY),
            pl.BlockSpec((1, d, tf), w_in_map),
            pl.BlockSpec((1, d, tf), w_in_map),
            pl.BlockSpec((1, tf, d), w_dn_map),
        ],
        out_specs=pl.BlockSpec(memory_space=pl.ANY),
        scratch_shapes=[
            pltpu.VMEM((2, tm, d), F32),
            pltpu.VMEM((tm, d), BF16),
            pltpu.VMEM((2, tm, d), F32),
            pltpu.SemaphoreType.DMA((2,)),
            pltpu.SemaphoreType.DMA((2,)),
        ],
    )
    return pl.pallas_call(
        functools.partial(_moe_kernel, tm=tm, unroll=8 if tm % 8 == 0 else 1, rows_per_step=tm // nj),
        grid_spec=grid_spec,
        out_shape=jax.ShapeDtypeStruct((2 * t, d), F32),
        compiler_params=_cparams(("arbitrary", "arbitrary")),
        name="moe_experts",
    )(tile_expert, n_active, n_valid, tok_of_slot, tok_of_slot, dst_of_slot, dst_of_slot, xn, wg, wu, wd)


def moe_dispatch_tables(info, *, tm, n_experts):
    t = info.shape[0]
    eidx = info[:, 2:4].astype(jnp.int32).reshape(-1)
    onehot = (eidx[:, None] == jnp.arange(n_experts, dtype=jnp.int32)[None, :]).astype(jnp.int32)
    csum = jnp.cumsum(onehot, axis=0)
    rank = jnp.sum((csum - onehot) * onehot, axis=1)
    counts = csum[-1]
    ntile_e = (counts + tm - 1) // tm
    tile_end = jnp.cumsum(ntile_e)
    tile_start = tile_end - ntile_e
    n_active = tile_end[-1]
    n_tiles = (2 * t) // tm + n_experts
    n_slots = n_tiles * tm
    slot = tile_start[eidx] * tm + rank
    pair = jnp.arange(2 * t, dtype=jnp.int32)
    tok_of_slot = jnp.zeros((n_slots,), jnp.int32).at[slot].set(pair // 2)
    dst_of_slot = jnp.zeros((n_slots,), jnp.int32).at[slot].set((pair % 2) * t + pair // 2)
    tiles = jnp.arange(n_tiles, dtype=jnp.int32)
    te = jnp.searchsorted(tile_end, jnp.minimum(tiles, n_active - 1), side="right").astype(jnp.int32)
    te = jnp.minimum(te, n_experts - 1)
    n_valid = jnp.clip(counts[te] - (tiles - tile_start[te]) * tm, 0, tm)
    n_valid = jnp.where(tiles < n_active, n_valid, 0).astype(jnp.int32)
    return (te, n_active.reshape(1).astype(jnp.int32), n_valid, tok_of_slot.reshape(n_tiles, 1, tm),
            dst_of_slot.reshape(n_tiles, 1, tm))


def _final_kernel(h_ref, y0_ref, y1_ref, info_ref, g_ref, o_ref):
    info = info_ref[...]
    x = h_ref[...] + info[:, 0:1] * y0_ref[...] + info[:, 1:2] * y1_ref[...]
    ms = jnp.mean(x * x, axis=-1, keepdims=True)
    o_ref[...] = x * lax.rsqrt(ms + EPS) * g_ref[...]


def final_combine(h, y, info, gain, *, tm=512):
    t, d = h.shape
    tm = min(tm, t)
    nt = t // tm
    return pl.pallas_call(
        _final_kernel,
        grid=(nt,),
        in_specs=[
            pl.BlockSpec((tm, d), lambda i: (i, 0)),
            pl.BlockSpec((tm, d), lambda i: (i, 0)),
            pl.BlockSpec((tm, d), lambda i: (nt + i, 0)),
            pl.BlockSpec((tm, LANES), lambda i: (i, 0)),
            pl.BlockSpec((1, d), lambda i: (0, 0)),
        ],
        out_specs=pl.BlockSpec((tm, d), lambda i: (i, 0)),
        out_shape=jax.ShapeDtypeStruct((t, d), F32),
        compiler_params=_cparams(("parallel",)),
        name="final_combine",
    )(h, y, y, info, gain.reshape(1, d).astype(F32))


def _softplus(x):
    return jnp.maximum(x, 0.0) + jnp.log1p(jnp.exp(-jnp.abs(x)))


def _gates_kernel(g_ref, p0_ref, p1_ref, col_ref, row_ref, *, mode, n_heads, rows, row_lo):
    x = g_ref[...]
    lane = lax.broadcasted_iota(jnp.int32, x.shape, 1)
    if mode == "gdn":
        logdec = -jnp.exp(p0_ref[...]) * _softplus(x + p1_ref[...])
        second = jax.nn.sigmoid(x)
    else:
        logdec = -_softplus(-(x + p0_ref[...]))
        second = x + p1_ref[...]
    logdec = jnp.where(lane < 2 * n_heads, logdec, 0.0)
    ri = lax.broadcasted_iota(jnp.int32, (CHUNK, CHUNK), 0)
    ci = lax.broadcasted_iota(jnp.int32, (CHUNK, CHUNK), 1)
    lower = (ri >= ci).astype(F32)
    upper = (ri <= ci).astype(F32)
    lane_c = lax.broadcasted_iota(jnp.int32, (CHUNK, LANES), 1)
    fwd_lane = lane_c < n_heads
    for c in range(rows // CHUNK):
        sl = slice(c * CHUNK, (c + 1) * CHUNK)
        ld = logdec[sl]
        cum = jnp.where(fwd_lane, _dot_hi(lower, ld), _dot_hi(upper, ld))
        sec = second[sl]
        if mode == "mlstm":
            sec = pltpu.roll(cum, 2 * n_heads, axis=1) - sec
        out = jnp.where(lane_c < 2 * n_heads, cum, jnp.where(lane_c < 4 * n_heads, sec, 0.0))
        col_ref[sl, :] = out
        row_ref[c] = jnp.transpose(out)[row_lo:row_lo + 2 * n_heads, :]


def gate_prep(g, p0, p1, *, mode, n_heads, rows=512):
    t = g.shape[0]
    rows = min(rows, t)
    assert t % rows == 0 and rows % CHUNK == 0
    row_lo = 0 if mode == "gdn" else 2 * n_heads
    return pl.pallas_call(
        functools.partial(_gates_kernel, mode=mode, n_heads=n_heads, rows=rows, row_lo=row_lo),
        grid=(t // rows,),
        in_specs=[
            pl.BlockSpec((rows, LANES), lambda i: (i, 0)),
            pl.BlockSpec((1, LANES), lambda i: (0, 0)),
            pl.BlockSpec((1, LANES), lambda i: (0, 0)),
        ],
        out_specs=[
            pl.BlockSpec((rows, LANES), lambda i: (i, 0)),
            pl.BlockSpec((rows // CHUNK, 2 * n_heads, CHUNK), lambda i: (i, 0, 0)),
        ],
        out_shape=[jax.ShapeDtypeStruct((t, LANES), F32), jax.ShapeDtypeStruct((t // CHUNK, 2 * n_heads, CHUNK), F32)],
        compiler_params=_cparams(("parallel",)),
        name="gate_prep_" + mode,
    )(g, p0, p1)


def _lane_vec(*parts):
    flat = jnp.concatenate([p.reshape(-1).astype(F32) for p in parts])
    return jnp.zeros((1, LANES), F32).at[0, :flat.shape[0]].set(flat)


def _col(tile, lane, j):
    return jnp.sum(jnp.where(lane == j, tile, 0.0), axis=1, keepdims=True)


def _tri_masks(direction):
    ri = lax.broadcasted_iota(jnp.int32, (CHUNK, CHUNK), 0)
    ci = lax.broadcasted_iota(jnp.int32, (CHUNK, CHUNK), 1)
    if direction == 0:
        return ri >= ci, ri > ci
    return ri <= ci, ri < ci


def _unit_tri_inverses(ms):
    ri = lax.broadcasted_iota(jnp.int32, (CHUNK, CHUNK), 0)
    ci = lax.broadcasted_iota(jnp.int32, (CHUNK, CHUNK), 1)
    eye = jnp.where(ri == ci, 1.0, 0.0)
    blk = ri // SUB == ci // SUB
    ds = [jnp.where(blk, m, 0.0) for m in ms]
    offs = [m - d for m, d in zip(ms, ds)]
    ps = [eye - d for d in ds]
    dps = ds
    for _ in range(3):
        dps = [_mm16(dp, dp) for dp in dps]
        ps = [p + _mm16(p, dp) for p, dp in zip(ps, dps)]
    ns = [_mm16(p, off) for p, off in zip(ps, offs)]
    n2s = [_mm16(n, n) for n in ns]
    qs = [(eye - n) + _mm16(eye - n, n2) for n, n2 in zip(ns, n2s)]
    return [_mm16(q, p) for q, p in zip(qs, ps)]


def _gdn_kernel(q_ref, k_ref, v_ref, z_ref, cwq_ref, cwk_ref, cwv_ref, gc_ref, gr_ref, gn_ref, o_ref,
                pad_ref, qn_ref, kn_ref, vn_ref, aq_ref, b_ref, egl_ref, oacc_ref,
                *, seq, n_heads, conv_rows, dk, group):
    h = pl.program_id(1)
    n_chunks = seq // CHUNK
    half = GDN_CONV // 2

    for src, cw_ref, dst, kind in ((q_ref, cwq_ref, qn_ref, "q"), (k_ref, cwk_ref, kn_ref, "k"), (v_ref, cwv_ref, vn_ref, "v")):
        pad_ref[0:8, :] = jnp.zeros((8, LANES), F32)
        pad_ref[seq + 8:seq + 16, :] = jnp.zeros((8, LANES), F32)
        pad_ref[8:seq + 8, :] = src[0].astype(F32)
        w = cw_ref[...]

        def conv_block(i, c, w=w, dst=dst, kind=kind):
            t0 = pl.multiple_of(i * conv_rows, conv_rows)
            win = pad_ref[pl.ds(t0, conv_rows + 16), :]
            acc = jnp.zeros((conv_rows, LANES), F32)
            for j in range(GDN_CONV):
                off = 8 - half + j
                acc = acc + win[off:off + conv_rows, :] * w[j:j + 1, :]
            y = _silu(acc)
            if kind != "v":
                y = y * lax.rsqrt(jnp.sum(y * y, axis=-1, keepdims=True) + EPS)
            if kind == "q":
                y = y * (dk ** -0.5)
            dst[pl.ds(t0, conv_rows), :] = y
            return c

        lax.fori_loop(0, seq // conv_rows, conv_block, 0)

    oacc_ref[...] = jnp.zeros_like(oacc_ref)
    lane = lax.broadcasted_iota(jnp.int32, (CHUNK, LANES), 1)

    def prepare(g, carry):
        chunks = []
        for j in range(group):
            c = g * group + j
            r0 = pl.multiple_of(c * CHUNK, CHUNK)
            qc = qn_ref[pl.ds(r0, CHUNK), :]
            kc = kn_ref[pl.ds(r0, CHUNK), :]
            chunks.append(dict(c=c, r0=r0, tile=gc_ref[0, pl.ds(r0, CHUNK), :], qc=qc, kc=kc,
                               vc=vn_ref[pl.ds(r0, CHUNK), :], q16=qc.astype(BF16), k16=kc.astype(BF16)))
        for ch in chunks:
            ch["kk"] = _dot_nt(ch["k16"], ch["k16"])
        for ch in chunks:
            ch["qk"] = _dot_nt(ch["q16"], ch["k16"])
        chains = []
        for ch in chunks:
            kt = jnp.transpose(ch["kc"])
            for direction in range(2):
                jd = direction * n_heads + h
                incl, strict = _tri_masks(direction)
                gcol = _col(ch["tile"], lane, jd)
                bcol = _col(ch["tile"], lane, 2 * n_heads + jd)
                grow = gr_ref[0, ch["c"], pl.ds(jd, 1), :]
                glast = gcol[CHUNK - 1:CHUNK] if direction == 0 else gcol[0:1]
                gam = jnp.exp(jnp.where(incl, gcol - grow, -jnp.inf))
                eg = jnp.exp(gcol)
                rhs = jnp.concatenate([ch["vc"] * bcol, ch["kc"] * (bcol * eg)], axis=1)
                rhs_hi = rhs.astype(BF16)
                chains.append(dict(
                    ch=ch, direction=direction, m=jnp.where(strict, ch["kk"] * bcol * gam, 0.0),
                    rhs_hi=rhs_hi, rhs_lo=(rhs - rhs_hi.astype(F32)).astype(BF16),
                    lhs=jnp.concatenate([(kt * jnp.exp(glast - grow)).astype(BF16), (ch["qk"] * gam).astype(BF16)], axis=0),
                    qd=ch["qc"] * eg, egl=jnp.exp(glast)))
        tinvs = [t.astype(BF16) for t in _unit_tri_inverses([cn["m"] for cn in chains])]
        sols = [_dot(t, cn["rhs_hi"]) + _dot(t, cn["rhs_lo"]) for t, cn in zip(tinvs, chains)]
        for cn, sol in zip(chains, sols):
            uw = jnp.concatenate([sol[:, :LANES], -sol[:, LANES:]], axis=1).astype(BF16)
            cn["r"] = _dot(cn["lhs"], uw)
        for cn in chains:
            d, c, r = cn["direction"], cn["ch"]["c"], cn["r"]
            aq_ref[d, c, 0:dk, :] = r[:dk, LANES:].astype(BF16)
            aq_ref[d, c, dk:dk + CHUNK, :] = (cn["qd"] + r[dk:, LANES:]).astype(BF16)
            b_ref[d, c] = r[:dk, :LANES].astype(BF16)
            egl_ref[d, c] = jnp.broadcast_to(cn["egl"], (8, LANES))
            oacc_ref[pl.ds(cn["ch"]["r0"], CHUNK), :] += r[dk:, :LANES]
        return carry

    lax.fori_loop(0, n_chunks // group, prepare, 0)

    def scan_step(c, direction, state):
        r0 = pl.multiple_of(c * CHUNK, CHUNK)
        r = _dot(aq_ref[direction, c], state.astype(BF16))
        oacc_ref[pl.ds(r0, CHUNK), :] += r[dk:]
        return state * egl_ref[direction, c][0:1, :] + r[:dk] + b_ref[direction, c].astype(F32)

    def scan(i, carry):
        sf, sb = carry
        return scan_step(i, 0, sf), scan_step(n_chunks - 1 - i, 1, sb)

    zero = jnp.zeros((dk, LANES), F32)
    lax.fori_loop(0, n_chunks, scan, (zero, zero))

    def epilogue(i, c):
        t0 = pl.multiple_of(i * conv_rows, conv_rows)
        o = oacc_ref[pl.ds(t0, conv_rows), :]
        ms = jnp.mean(o * o, axis=-1, keepdims=True)
        y = o * lax.rsqrt(ms + EPS) * gn_ref[...]
        z = z_ref[0, pl.ds(t0, conv_rows), :].astype(F32)
        o_ref[0, pl.ds(t0, conv_rows), :] = (y * _silu(z)).astype(o_ref.dtype)
        return c

    lax.fori_loop(0, seq // conv_rows, epilogue, 0)


def gdn_mixer(proj, conv_w, gcol, grow, gdn_norm, *, batch, seq, n_heads, col0, conv_rows=256, group=8):
    hb = n_heads
    conv_rows = min(conv_rows, seq)
    n_chunks = seq // CHUNK
    dk = LANES
    group = math.gcd(group, n_chunks)

    def head_spec(group):
        return pl.BlockSpec((1, seq, LANES), lambda b, h, group=group: (b, 0, col0 + group * hb + h))

    def conv_spec(group):
        return pl.BlockSpec((GDN_CONV, LANES), lambda b, h, group=group: (0, group * hb + h))

    return pl.pallas_call(
        functools.partial(_gdn_kernel, seq=seq, n_heads=n_heads, conv_rows=conv_rows, dk=dk, group=group),
        grid=(batch, n_heads),
        in_specs=[
            head_spec(0), head_spec(1), head_spec(2), head_spec(3),
            conv_spec(0), conv_spec(1), conv_spec(2),
            pl.BlockSpec((1, seq, LANES), lambda b, h: (b, 0, 0)),
            pl.BlockSpec((1, n_chunks, 2 * n_heads, CHUNK), lambda b, h: (b, 0, 0, 0)),
            pl.BlockSpec((1, LANES), lambda b, h: (0, 0)),
        ],
        out_specs=pl.BlockSpec((1, seq, LANES), lambda b, h: (b, 0, h)),
        out_shape=jax.ShapeDtypeStruct((batch, seq, n_heads * LANES), BF16),
        scratch_shapes=[
            pltpu.VMEM((seq + 16, LANES), F32),
            pltpu.VMEM((seq, LANES), F32),
            pltpu.VMEM((seq, LANES), F32),
            pltpu.VMEM((seq, LANES), F32),
            pltpu.VMEM((2, n_chunks, dk + CHUNK, LANES), BF16),
            pltpu.VMEM((2, n_chunks, dk, LANES), BF16),
            pltpu.VMEM((2, n_chunks, 8, LANES), F32),
            pltpu.VMEM((seq, LANES), F32),
        ],
        compiler_params=_cparams(("parallel", "arbitrary")),
        name="gdn_mixer",
    )(proj, proj, proj, proj, conv_w, conv_w, conv_w,
      gcol.reshape(batch, seq, LANES), grow.reshape(batch, n_chunks, 2 * n_heads, CHUNK),
      gdn_norm.reshape(1, LANES).astype(F32))


def _mlstm_kernel(q_ref, k_ref, v_ref, og_ref, gc_ref, gr_ref, nrm_ref, o_ref,
                  hf_ref, hb_ref, st_ref, *, seq, n_heads, rows, dqk, dv):
    p = pl.program_id(1)
    n_chunks = seq // CHUNK
    lane = lax.broadcasted_iota(jnp.int32, (CHUNK, LANES), 1)
    ones = jnp.ones((CHUNK, dv), BF16)
    scale = dqk ** -0.5

    def chunk(c, direction, hh, m):
        r0 = pl.multiple_of(c * CHUNK, CHUNK)
        jd = direction * n_heads + 2 * p + hh
        incl, _ = _tri_masks(direction)
        tile = gc_ref[0, pl.ds(r0, CHUNK), :]
        bc = _col(tile, lane, jd)
        ac = _col(tile, lane, 2 * n_heads + jd)
        arow = gr_ref[0, c, pl.ds(jd, 1), :]
        blast = bc[CHUNK - 1:CHUNK] if direction == 0 else bc[0:1]
        dmat = jnp.where(incl, bc - arow, NEG)
        dmax = jnp.max(dmat, axis=1, keepdims=True)
        q = q_ref[0, pl.ds(r0, CHUNK), hh * dqk:(hh + 1) * dqk]
        k = k_ref[0, pl.ds(r0, CHUNK), hh * dqk:(hh + 1) * dqk]
        vaug = jnp.concatenate([v_ref[0, pl.ds(r0, CHUNK), hh * dv:(hh + 1) * dv], ones], axis=1)
        qk = _dot_nt(q, k) * scale
        inter = bc + m
        mj = jnp.maximum(inter, dmax)
        s = qk * jnp.exp(dmat - mj)
        iw = jnp.exp(inter - mj)
        idx = direction * 2 + hh
        state = st_ref[idx]
        num = iw * _dot(q, state.astype(BF16)) + _dot(s.astype(BF16), vaug)
        den = num[:, dv:dv + 1]
        hout = num[:, :dv] / jnp.maximum(jnp.abs(den), jnp.exp(-mj))
        wl = blast - ac
        m_new = jnp.maximum(blast + m, jnp.max(wl, axis=0, keepdims=True))
        ws = jnp.exp(wl - m_new)
        decay = jnp.exp(blast + m - m_new)
        kw = (k.astype(F32) * (ws * scale)).astype(BF16)
        st_ref[idx] = decay * state + _dot_tn(kw, vaug)
        return r0, hout, m_new

    st_ref[...] = jnp.zeros_like(st_ref)

    def body(i, ms):
        out = []
        for direction in range(2):
            c = i if direction == 0 else n_chunks - 1 - i
            dst = hf_ref if direction == 0 else hb_ref
            for hh in range(2):
                r0, hout, m_new = chunk(c, direction, hh, ms[direction * 2 + hh])
                dst[pl.ds(r0, CHUNK), hh * dv:(hh + 1) * dv] = hout
                out.append(m_new)
        return tuple(out)

    zero = jnp.zeros((1, 1), F32)
    lax.fori_loop(0, n_chunks, body, (zero, zero, zero, zero))

    def epilogue(i, c):
        t0 = pl.multiple_of(i * rows, rows)
        g = nrm_ref[0]
        og = og_ref[0, pl.ds(t0, rows), :].astype(F32)
        for hh in range(2):
            sl = slice(hh * dv, (hh + 1) * dv)
            x = hf_ref[pl.ds(t0, rows), sl] + hb_ref[pl.ds(t0, rows), sl]
            ms = jnp.mean(x * x, axis=-1, keepdims=True)
            y = x * lax.rsqrt(ms + EPS) * g[:, sl]
            o_ref[0, pl.ds(t0, rows), sl] = (y * jax.nn.sigmoid(og[:, sl])).astype(o_ref.dtype)
        return c

    lax.fori_loop(0, seq // rows, epilogue, 0)


def mlstm_mixer(proj, gcol, grow, mlstm_norm, *, batch, seq, n_heads, q_blk, k_blk, v_blk, o_blk, rows=256):
    dqk, dv = MLSTM_DQK, MLSTM_DV
    n_chunks = seq // CHUNK
    rows = min(rows, seq)
    npair = n_heads // 2
    return pl.pallas_call(
        functools.partial(_mlstm_kernel, seq=seq, n_heads=n_heads, rows=rows, dqk=dqk, dv=dv),
        grid=(batch, npair),
        in_specs=[
            pl.BlockSpec((1, seq, 2 * dqk), lambda b, p: (b, 0, q_blk + p)),
            pl.BlockSpec((1, seq, 2 * dqk), lambda b, p: (b, 0, k_blk + p)),
            pl.BlockSpec((1, seq, 2 * dv), lambda b, p: (b, 0, v_blk + p)),
            pl.BlockSpec((1, seq, 2 * dv), lambda b, p: (b, 0, o_blk + p)),
            pl.BlockSpec((1, seq, LANES), lambda b, p: (b, 0, 0)),
            pl.BlockSpec((1, n_chunks, 2 * n_heads, CHUNK), lambda b, p: (b, 0, 0, 0)),
            pl.BlockSpec((1, 1, 2 * dv), lambda b, p: (p, 0, 0)),
        ],
        out_specs=pl.BlockSpec((1, seq, 2 * dv), lambda b, p: (b, 0, p)),
        out_shape=jax.ShapeDtypeStruct((batch, seq, n_heads * dv), BF16),
        scratch_shapes=[
            pltpu.VMEM((seq, 2 * dv), F32),
            pltpu.VMEM((seq, 2 * dv), F32),
            pltpu.VMEM((4, dqk, 2 * dv), F32),
        ],
        compiler_params=_cparams(("parallel", "arbitrary")),
        name="mlstm_mixer",
    )(proj, proj, proj, proj, gcol.reshape(batch, seq, LANES), grow.reshape(batch, n_chunks, 2 * n_heads, CHUNK),
      mlstm_norm.reshape(npair, 1, 2 * dv).astype(F32))


def _dsw_kernel(slope_ref, q_ref, k_ref, v_ref, o_ref, l_ref, *, u_len, dil, half, n_heads, qb):
    h = pl.program_id(1) % n_heads
    bias_step = slope_ref[h] * float(dil)
    kb = qb + 2 * half
    scale = LANES ** -0.5

    def block(i, c):
        u0 = pl.multiple_of(i * qb, qb)
        ks = jnp.clip(u0 - half, 0, u_len - kb)
        ks = pl.multiple_of(ks, half)
        q = q_ref[0, pl.ds(u0, qb), :]
        k = k_ref[0, pl.ds(ks, kb), :]
        v = v_ref[0, pl.ds(ks, kb), :]
        s = _dot_nt(q, k) * scale
        qpos = u0 + lax.broadcasted_iota(jnp.int32, (qb, kb), 0)
        kpos = ks + lax.broadcasted_iota(jnp.int32, (qb, kb), 1)
        dist = jnp.abs(kpos - qpos)
        s = s - bias_step * dist.astype(F32)
        s = jnp.where(dist <= half, s, NEG)
        mx = jnp.max(s, axis=-1, keepdims=True)
        p = jnp.exp(s - mx)
        den = jnp.sum(p, axis=-1, keepdims=True)
        o = _dot(p.astype(BF16), v) / den
        o_ref[0, pl.ds(u0, qb), :] = o
        l_ref[0, pl.ds(u0, qb), :] = jnp.broadcast_to(mx + jnp.log(den), (qb, LANES))
        return c

    lax.fori_loop(0, u_len // qb, block, 0)


def dsw_branch(proj, slopes, *, batch, seq, width, n_heads, q_blk, k_blk, v_blk, window, dil, qb=128):
    u_len = seq // dil
    half = window // (2 * dil)
    wb = width // LANES
    view = proj.reshape(batch, u_len, dil * width)

    def spec(blk):
        return pl.BlockSpec((1, u_len, LANES), lambda b, j, s, blk=blk: (b, 0, (j // n_heads) * wb + blk + j % n_heads))

    out_spec = pl.BlockSpec((1, u_len, LANES), lambda b, j, s: (b, 0, j))
    shape = jax.ShapeDtypeStruct((batch, u_len, dil * n_heads * LANES), F32)
    o, l = pl.pallas_call(
        functools.partial(_dsw_kernel, u_len=u_len, dil=dil, half=half, n_heads=n_heads, qb=qb),
        grid_spec=pltpu.PrefetchScalarGridSpec(
            num_scalar_prefetch=1,
            grid=(batch, dil * n_heads),
            in_specs=[spec(q_blk), spec(k_blk), spec(v_blk)],
            out_specs=[out_spec, out_spec],
        ),
        out_shape=[shape, shape],
        compiler_params=_cparams(("parallel", "arbitrary")),
        name="dsw_branch_r%d" % dil,
    )(slopes, view, view, view)
    return o.reshape(batch * seq, n_heads * LANES), l.reshape(batch * seq, n_heads * LANES)


def _dsw_merge_kernel(o1, l1, o2, l2, o3, l3, out_ref):
    a, b, c = l1[...], l2[...], l3[...]
    mx = jnp.maximum(jnp.maximum(a, b), c)
    ea, eb, ec = jnp.exp(a - mx), jnp.exp(b - mx), jnp.exp(c - mx)
    tot = ea + eb + ec
    out_ref[...] = ((o1[...] * ea + o2[...] * eb + o3[...] * ec) / tot).astype(out_ref.dtype)


def dsw_merge(parts, *, tm=512):
    t, w = parts[0][0].shape
    tm = min(tm, t)
    flat = [a for pair in parts for a in pair]
    spec = pl.BlockSpec((tm, w), lambda i: (i, 0))
    return pl.pallas_call(
        _dsw_merge_kernel,
        grid=(t // tm,),
        in_specs=[spec] * 6,
        out_specs=spec,
        out_shape=jax.ShapeDtypeStruct((t, w), BF16),
        compiler_params=_cparams(("parallel",)),
        name="dsw_merge",
    )(*flat)


def _mla_kernel(q_ref, tq_ref, kn_ref, v_ref, kr_ref, krr_ref, ck_ref, sk_ref, o_ref, k_scr, v_scr, *, seq, kvb):
    @pl.when(pl.program_id(2) == 0)
    def _():
        k_scr[:, :LANES] = kn_ref[0]
        rk = kr_ref[0].astype(F32) * ck_ref[...] + krr_ref[0].astype(F32) * sk_ref[...]
        k_scr[:, LANES:] = rk.astype(BF16)
        v_scr[:, :LANES] = v_ref[0]
        v_scr[:, LANES:] = jnp.ones((seq, LANES), BF16)

    q = (q_ref[0].astype(F32) * tq_ref[...]).astype(BF16)
    mx = jnp.full((q.shape[0], 1), -jnp.inf, F32)
    acc = jnp.zeros((q.shape[0], 2 * LANES), F32)
    for b in range(seq // kvb):
        s = _dot_nt(q, k_scr[b * kvb:(b + 1) * kvb, :])
        mx_new = jnp.maximum(mx, jnp.max(s, axis=-1, keepdims=True))
        p = jnp.exp2(s - mx_new).astype(BF16)
        acc = acc * jnp.exp2(mx - mx_new) + _dot(p, v_scr[b * kvb:(b + 1) * kvb, :])
        mx = mx_new
    o_ref[0] = (acc[:, :LANES] / acc[:, LANES:LANES + 1]).astype(o_ref.dtype)


def mla_attention(q_up, kv_up, proj, tq_tab, ck_tab, sk_tab, *, batch, seq, n_heads, kr_blk, krr_blk, tq=512, kvb=512):
    tq = min(tq, seq)
    kvb = min(kvb, seq)
    return pl.pallas_call(
        functools.partial(_mla_kernel, seq=seq, kvb=kvb),
        grid=(batch, n_heads, seq // tq),
        in_specs=[
            pl.BlockSpec((1, tq, 2 * LANES), lambda b, h, i: (b, i, h)),
            pl.BlockSpec((tq, 2 * LANES), lambda b, h, i: (i, 0)),
            pl.BlockSpec((1, seq, LANES), lambda b, h, i: (b, 0, 2 * h)),
            pl.BlockSpec((1, seq, LANES), lambda b, h, i: (b, 0, 2 * h + 1)),
            pl.BlockSpec((1, seq, LANES), lambda b, h, i: (b, 0, kr_blk)),
            pl.BlockSpec((1, seq, LANES), lambda b, h, i: (b, 0, krr_blk)),
            pl.BlockSpec((seq, LANES), lambda b, h, i: (0, 0)),
            pl.BlockSpec((seq, LANES), lambda b, h, i: (0, 0)),
        ],
        out_specs=pl.BlockSpec((1, tq, LANES), lambda b, h, i: (b, i, h)),
        out_shape=jax.ShapeDtypeStruct((batch, seq, n_heads * LANES), BF16),
        scratch_shapes=[pltpu.VMEM((seq, 2 * LANES), BF16), pltpu.VMEM((seq, 2 * LANES), BF16)],
        compiler_params=_cparams(("parallel", "parallel", "arbitrary")),
        name="mla_attention",
    )(q_up, tq_tab, kv_up, kv_up, proj, proj, ck_tab, sk_tab)


def _rot_cols(w):
    half = w.shape[-1] // 2
    return jnp.concatenate([-w[..., half:], w[..., :half]], axis=-1)


def layer_even(h, batch, seq, mix_norm, w_in, conv_w, a_log, dt_bias, gdn_norm, w_out, ffn_norm, w_gate, w_up, w_down,
               *, tm=512):
    d = h.shape[1]
    nh = GDN_HEADS
    qk = nh * LANES
    gate0 = 4 * qk
    att0 = gate0 + 4 * nh
    w_gates = jnp.zeros((d, LANES), F32).at[:, :4 * nh].set(w_in[:, gate0:att0]).astype(BF16)
    proj_a = norm_matmul(h, mix_norm, w_in[:, :gate0].astype(BF16), out_dtype=BF16, tm=tm, name="l0_in_proj_gdn")
    proj_b = norm_matmul(h, mix_norm, w_in[:, att0:].astype(BF16), out_dtype=BF16, tm=tm, name="l0_in_proj_dsw")
    graw = norm_matmul(h, mix_norm, w_gates, out_dtype=F32, tm=tm, name="l0_gate_proj")
    gcol, grow = gate_prep(graw, _lane_vec(a_log), _lane_vec(dt_bias), mode="gdn", n_heads=nh)
    o_a = gdn_mixer(proj_a.reshape(batch, seq, gate0), conv_w.astype(F32), gcol, grow, gdn_norm,
                    batch=batch, seq=seq, n_heads=nh, col0=0)
    slopes = (2.0 ** (-8.0 * jnp.arange(1, DSW_HEADS + 1, dtype=F32) / DSW_HEADS)).astype(F32)
    width_b = proj_b.shape[1]
    proj_b3 = proj_b.reshape(batch, seq, width_b)
    parts = [dsw_branch(proj_b3, slopes, batch=batch, seq=seq, width=width_b, n_heads=DSW_HEADS,
                        q_blk=0, k_blk=DSW_HEADS, v_blk=2 * DSW_HEADS, window=win, dil=r)
             for win, r in DSW_PAIRS]
    o_b = dsw_merge(parts, tm=tm)
    na = nh * LANES
    h = proj_residual(o_a.reshape(batch * seq, na), o_b, w_out[:na].astype(BF16), w_out[na:].astype(BF16), h,
                      tm=tm, name="l0_out_proj")
    return ffn_residual(h, ffn_norm, w_gate.astype(BF16), w_up.astype(BF16), w_down.astype(BF16), tm=tm)


def layer_odd_mixers(h, batch, seq, mix_norm, w_in, q_norm, kv_norm, w_uq, w_ukv, ig_bias, fg_bias, mlstm_norm, w_out,
                     *, tm=512):
    d = h.shape[1]
    nh = MLSTM_HEADS
    sizes = (MLA_Q_RANK, MLA_KV_RANK, MLA_ROPE, nh * MLSTM_DQK, nh * MLSTM_DQK, nh * MLSTM_DV, nh * MLSTM_DV, 2 * nh, 2 * nh)
    cuts = np.cumsum((0,) + sizes)
    w_cq, w_ckv, w_kr, w_mq, w_mk, w_mv, w_mo, w_mi, w_mf = (w_in[:, cuts[i]:cuts[i + 1]] for i in range(9))
    w_krot = _rot_cols(w_kr)
    w_main = jnp.concatenate([w_cq, w_kr, w_kr, w_krot, w_krot, w_ckv, w_mq, w_mk, w_mv, w_mo], axis=1).astype(BF16)
    w_gates = jnp.zeros((d, LANES), F32).at[:, :2 * nh].set(w_mf).at[:, 2 * nh:4 * nh].set(w_mi).astype(BF16)
    proj = norm_matmul(h, mix_norm, w_main, out_dtype=BF16, tm=tm, name="l1_in_proj")
    graw = norm_matmul(h, mix_norm, w_gates, out_dtype=F32, tm=tm, name="l1_gate_proj")
    gcol, grow = gate_prep(graw, _lane_vec(fg_bias), _lane_vec(jnp.zeros((2 * nh,), F32), ig_bias), mode="mlstm", n_heads=nh)
    width = proj.shape[1]
    proj3 = proj.reshape(batch, seq, width)

    hq = MLA_HEADS
    wq = w_uq.reshape(MLA_Q_RANK, hq, MLA_NOPE + MLA_ROPE)
    wq_rope = wq[:, :, MLA_NOPE:]
    wq_all = jnp.concatenate([wq[:, :, :MLA_NOPE], wq_rope, _rot_cols(wq_rope)], axis=-1).reshape(MLA_Q_RANK, hq * 2 * LANES)
    q_up = norm_matmul(proj, q_norm, wq_all.astype(BF16), out_dtype=BF16, k_block=0, k_width=MLA_Q_RANK, tm=tm, name="mla_q_up")
    kv_blk = (MLA_Q_RANK + 2 * LANES) // MLA_KV_RANK
    kv_up = norm_matmul(proj, kv_norm, w_ukv.astype(BF16), out_dtype=BF16, k_block=kv_blk, k_width=MLA_KV_RANK, tm=tm, name="mla_kv_up")
    pos = jnp.arange(seq, dtype=F32)
    freqs = ROPE_THETA ** (-jnp.arange(0, MLA_ROPE, 2, dtype=F32) / MLA_ROPE)
    ang = pos[:, None] * freqs[None, :]
    cos, sin = jnp.cos(ang), jnp.sin(ang)
    scale = LOG2E * (MLA_NOPE + MLA_ROPE) ** -0.5
    tq_tab = scale * jnp.concatenate([jnp.ones((seq, MLA_NOPE), F32), cos, cos, sin, sin], axis=1)
    ck_tab = jnp.concatenate([cos] * 4, axis=1)
    sk_tab = jnp.concatenate([sin] * 4, axis=1)
    kr_blk = MLA_Q_RANK // LANES
    o_c = mla_attention(q_up.reshape(batch, seq, -1), kv_up.reshape(batch, seq, -1), proj3, tq_tab, ck_tab, sk_tab,
                        batch=batch, seq=seq, n_heads=hq, kr_blk=kr_blk, krr_blk=kr_blk + 1)

    mq0 = kr_blk + 2 + MLA_KV_RANK // LANES
    mk0 = mq0 + nh * MLSTM_DQK // LANES
    mv0 = mk0 + nh * MLSTM_DQK // LANES
    mo0 = mv0 + nh * MLSTM_DV // LANES
    o_d = mlstm_mixer(proj3, gcol, grow, mlstm_norm, batch=batch, seq=seq, n_heads=nh,
                      q_blk=mq0, k_blk=mk0, v_blk=mv0 // 2, o_blk=mo0 // 2)
    nc = hq * MLA_V
    return proj_residual(o_c.reshape(batch * seq, nc), o_d.reshape(batch * seq, -1), w_out[:nc].astype(BF16),
                         w_out[nc:].astype(BF16), h, tm=tm, name="l1_out_proj")


def moe_block(h, ffn_norm, w_router, b_router, we_gate, we_up, we_down, final_norm, *, tm=448, tf=512, tm_tok=512):
    t = h.shape[0]
    n_experts = we_gate.shape[0]
    tm = min(tm, t)
    xn, info = router(h, ffn_norm, w_router, b_router, tm=tm_tok)
    te, n_active, n_valid, tok_of_slot, dst_of_slot = moe_dispatch_tables(info, tm=tm, n_experts=n_experts)
    y = moe_experts(xn, te, n_active, n_valid, tok_of_slot, dst_of_slot, we_gate.astype(BF16), we_up.astype(BF16),
                    we_down.astype(BF16), tm=tm, tf=tf)
    return final_combine(h, y, info, final_norm, tm=tm_tok)


def kernel(x, even_mix_norm, even_w_in, even_conv_w, even_a_log, even_dt_bias, even_gdn_norm, even_w_out,
           even_ffn_norm, even_w_gate, even_w_up, even_w_down, odd_mix_norm, odd_w_in, odd_q_norm, odd_kv_norm,
           odd_w_uq, odd_w_ukv, odd_ig_bias, odd_fg_bias, odd_mlstm_norm, odd_w_out, odd_ffn_norm,
           odd_w_router, odd_b_router, odd_we_gate, odd_we_up, odd_we_down, final_norm):
    batch, seq, d = x.shape
    h = x.reshape(batch * seq, d)
    h = layer_even(h, batch, seq, even_mix_norm[0], even_w_in[0], even_conv_w[0], even_a_log[0], even_dt_bias[0],
                   even_gdn_norm[0], even_w_out[0], even_ffn_norm[0], even_w_gate[0], even_w_up[0], even_w_down[0])
    h = layer_odd_mixers(h, batch, seq, odd_mix_norm[0], odd_w_in[0], odd_q_norm[0], odd_kv_norm[0], odd_w_uq[0],
                         odd_w_ukv[0], odd_ig_bias[0], odd_fg_bias[0], odd_mlstm_norm[0], odd_w_out[0])
    out = moe_block(h, odd_ffn_norm[0], odd_w_router[0], odd_b_router[0], odd_we_gate[0], odd_we_up[0],
                    odd_we_down[0], final_norm)
    return out.reshape(batch, seq, d)
```

```python
import functools
import math

import jax
import jax.numpy as jnp
import numpy as np
from jax import lax
from jax.experimental import pallas as pl
from jax.experimental.pallas import tpu as pltpu

F32 = jnp.float32
BF16 = jnp.bfloat16
EPS = 1e-6
NEG = -1e30
HI = lax.Precision.HIGHEST
LOG2E = math.log2(math.e)

LANES = 128
CHUNK = 64
SUB = 16
VMEM_LIMIT = 56 * 1024 * 1024

GDN_HEADS = 8
GDN_CONV = 5
DSW_HEADS = 8
DSW_PAIRS = ((128, 1), (512, 4), (2048, 16))
MLA_HEADS = 8
MLA_Q_RANK = 768
MLA_KV_RANK = 512
MLA_NOPE = 128
MLA_ROPE = 64
MLA_V = 128
ROPE_THETA = 10000.0
MLSTM_HEADS = 8
MLSTM_DQK = 64
MLSTM_DV = 128


def _cparams(sem):
    return pltpu.CompilerParams(dimension_semantics=sem, vmem_limit_bytes=VMEM_LIMIT)


def _dot(a, b):
    return jnp.dot(a, b, preferred_element_type=F32)


def _dot_nt(a, b):
    return lax.dot_general(a, b, (((1,), (1,)), ((), ())), preferred_element_type=F32)


def _dot_tn(a, b):
    return lax.dot_general(a, b, (((0,), (0,)), ((), ())), preferred_element_type=F32)


def _dot_hi(a, b):
    return jnp.dot(a, b, preferred_element_type=F32, precision=HI)


def _mm16(a, b):
    return _dot(a.astype(BF16), b.astype(BF16))


def _nm_kernel(x_ref, g_ref, w_ref, o_ref, xn_ref, *, use_norm):
    @pl.when(pl.program_id(1) == 0)
    def _():
        x = x_ref[...].astype(F32)
        if use_norm:
            ms = jnp.mean(x * x, axis=-1, keepdims=True)
            x = x * lax.rsqrt(ms + EPS) * g_ref[...]
        xn_ref[...] = x.astype(BF16)

    o_ref[...] = _dot(xn_ref[...], w_ref[...]).astype(o_ref.dtype)


def norm_matmul(x, gain, w, *, out_dtype, k_block=0, k_width=None, tm=512, tn=512, use_norm=True, name="norm_matmul"):
    t = x.shape[0]
    kw = x.shape[1] if k_width is None else k_width
    n = w.shape[1]
    tm = min(tm, t)
    tn = min(tn, n)
    assert t % tm == 0 and n % tn == 0 and w.shape[0] == kw
    g2 = gain.reshape(1, kw).astype(F32)
    return pl.pallas_call(
        functools.partial(_nm_kernel, use_norm=use_norm),
        grid=(t // tm, n // tn),
        in_specs=[
            pl.BlockSpec((tm, kw), lambda i, j: (i, k_block)),
            pl.BlockSpec((1, kw), lambda i, j: (0, 0)),
            pl.BlockSpec((kw, tn), lambda i, j: (0, j)),
        ],
        out_specs=pl.BlockSpec((tm, tn), lambda i, j: (i, j)),
        out_shape=jax.ShapeDtypeStruct((t, n), out_dtype),
        scratch_shapes=[pltpu.VMEM((tm, kw), BF16)],
        compiler_params=_cparams(("parallel", "arbitrary")),
        name=name,
    )(x, g2, w)


def _proj_res_kernel(a_ref, b_ref, w1_ref, w2_ref, r_ref, o_ref):
    acc = _dot(a_ref[...], w1_ref[...]) + _dot(b_ref[...], w2_ref[...])
    o_ref[...] = r_ref[...] + acc


def proj_residual(a, b, w1, w2, res, *, tm=512, name="proj_residual"):
    t, ka = a.shape
    kb = b.shape[1]
    n = w1.shape[1]
    tm = min(tm, t)
    assert t % tm == 0
    return pl.pallas_call(
        _proj_res_kernel,
        grid=(t // tm,),
        in_specs=[
            pl.BlockSpec((tm, ka), lambda i: (i, 0)),
            pl.BlockSpec((tm, kb), lambda i: (i, 0)),
            pl.BlockSpec((ka, n), lambda i: (0, 0)),
            pl.BlockSpec((kb, n), lambda i: (0, 0)),
            pl.BlockSpec((tm, n), lambda i: (i, 0)),
        ],
        out_specs=pl.BlockSpec((tm, n), lambda i: (i, 0)),
        out_shape=jax.ShapeDtypeStruct((t, n), F32),
        compiler_params=_cparams(("parallel",)),
        name=name,
    )(a, b, w1, w2, res)


def _silu(a):
    return a * jax.nn.sigmoid(a)


def _ffn_kernel(h_ref, g_ref, wg_ref, wu_ref, wd_ref, o_ref, xn_ref, acc_ref):
    j = pl.program_id(1)

    @pl.when(j == 0)
    def _():
        x = h_ref[...]
        ms = jnp.mean(x * x, axis=-1, keepdims=True)
        xn_ref[...] = (x * lax.rsqrt(ms + EPS) * g_ref[...]).astype(BF16)
        acc_ref[...] = jnp.zeros_like(acc_ref)

    xn = xn_ref[...]
    mid = (_silu(_dot(xn, wg_ref[...])) * _dot(xn, wu_ref[...])).astype(BF16)
    acc_ref[...] += _dot(mid, wd_ref[...])

    @pl.when(j == pl.num_programs(1) - 1)
    def _():
        o_ref[...] = h_ref[...] + acc_ref[...]


def ffn_residual(h, gain, wg, wu, wd, *, tm=512, tf=512, name="ffn_swiglu"):
    t, d = h.shape
    f = wg.shape[1]
    tm = min(tm, t)
    tf = min(tf, f)
    assert t % tm == 0 and f % tf == 0
    return pl.pallas_call(
        _ffn_kernel,
        grid=(t // tm, f // tf),
        in_specs=[
            pl.BlockSpec((tm, d), lambda i, j: (i, 0)),
            pl.BlockSpec((1, d), lambda i, j: (0, 0)),
            pl.BlockSpec((d, tf), lambda i, j: (0, j)),
            pl.BlockSpec((d, tf), lambda i, j: (0, j)),
            pl.BlockSpec((tf, d), lambda i, j: (j, 0)),
        ],
        out_specs=pl.BlockSpec((tm, d), lambda i, j: (i, 0)),
        out_shape=jax.ShapeDtypeStruct((t, d), F32),
        scratch_shapes=[pltpu.VMEM((tm, d), BF16), pltpu.VMEM((tm, d), F32)],
        compiler_params=_cparams(("parallel", "arbitrary")),
        name=name,
    )(h, gain.reshape(1, d).astype(F32), wg, wu, wd)


def _router_kernel(h_ref, g_ref, wr_ref, br_ref, xn_ref, info_ref, *, n_experts):
    x = h_ref[...]
    ms = jnp.mean(x * x, axis=-1, keepdims=True)
    xn = x * lax.rsqrt(ms + EPS) * g_ref[...]
    xn_ref[...] = xn
    logits = _dot_hi(xn, wr_ref[...]) + br_ref[...]
    lane = lax.broadcasted_iota(jnp.int32, logits.shape, 1)
    real = lane < n_experts
    lg = jnp.where(real, logits, -jnp.inf)
    mx = jnp.max(lg, axis=-1, keepdims=True)
    ex = jnp.exp(lg - mx)
    probs = ex / jnp.sum(ex, axis=-1, keepdims=True)
    probs = jnp.where(real, probs, -1.0)
    v1 = jnp.max(probs, axis=-1, keepdims=True)
    i1 = jnp.min(jnp.where(probs == v1, lane, LANES), axis=-1, keepdims=True)
    rest = jnp.where(lane == i1, -1.0, probs)
    v2 = jnp.max(rest, axis=-1, keepdims=True)
    i2 = jnp.min(jnp.where(rest == v2, lane, LANES), axis=-1, keepdims=True)
    tot = v1 + v2
    info = jnp.where(lane == 0, v1 / tot, 0.0)
    info = jnp.where(lane == 1, v2 / tot, info)
    info = jnp.where(lane == 2, i1.astype(F32), info)
    info = jnp.where(lane == 3, i2.astype(F32), info)
    info_ref[...] = info


def router(h, gain, w_router, b_router, *, tm=512):
    t, d = h.shape
    e = w_router.shape[1]
    tm = min(tm, t)
    wr = jnp.zeros((d, LANES), F32).at[:, :e].set(w_router.astype(F32))
    br = jnp.zeros((1, LANES), F32).at[0, :e].set(b_router.astype(F32))
    return pl.pallas_call(
        functools.partial(_router_kernel, n_experts=e),
        grid=(t // tm,),
        in_specs=[
            pl.BlockSpec((tm, d), lambda i: (i, 0)),
            pl.BlockSpec((1, d), lambda i: (0, 0)),
            pl.BlockSpec((d, LANES), lambda i: (0, 0)),
            pl.BlockSpec((1, LANES), lambda i: (0, 0)),
        ],
        out_specs=[pl.BlockSpec((tm, d), lambda i: (i, 0)), pl.BlockSpec((tm, LANES), lambda i: (i, 0))],
        out_shape=[jax.ShapeDtypeStruct((t, d), F32), jax.ShapeDtypeStruct((t, LANES), F32)],
        compiler_params=_cparams(("parallel",)),
        name="router",
    )(h, gain.reshape(1, d).astype(F32), wr, br)


def _moe_kernel(te_ref, na_ref, nv_ref, tok_ref, tokn_ref, dst_ref, dstp_ref, xn_hbm, wg_ref, wu_ref, wd_ref, out_hbm,
                xrow_ref, xb_ref, acc_ref, sem_in, sem_out, *, tm, unroll, rows_per_step):
    i = pl.program_id(0)
    j = pl.program_id(1)
    nj = pl.num_programs(1)
    n_active = na_ref[0]
    active = i < n_active
    slot = i % 2
    n_valid = nv_ref[i]
    n_valid_prev = nv_ref[jnp.maximum(i - 1, 0)]

    def gather_copy(ids_ref, r, s):
        return pltpu.make_async_copy(xn_hbm.at[pl.ds(ids_ref[0, 0, r], 1)], xrow_ref.at[s, pl.ds(r, 1)], sem_in.at[s])

    def scatter_copy(ids_ref, r, s):
        return pltpu.make_async_copy(acc_ref.at[s, pl.ds(r, 1)], out_hbm.at[pl.ds(ids_ref[0, 0, r], 1)], sem_out.at[s])

    def for_rows(n, fn, unroll=1):
        def body(r, c):
            fn(r)
            return c

        if isinstance(n, int):
            lax.fori_loop(0, n, body, 0, unroll=unroll)
            return

        def body_blk(b, c):
            for u in range(unroll):
                fn(b * unroll + u)
            return c

        n_blk = n // unroll
        lax.fori_loop(0, n_blk, body_blk, 0)
        lax.fori_loop(n_blk * unroll, n, body, 0)

    @pl.when(active & (j == 0))
    def _():
        @pl.when(i == 0)
        def _():
            for_rows(tm, lambda r: gather_copy(tok_ref, r, 0).start(), unroll)

        for_rows(tm, lambda r: gather_copy(tok_ref, r, slot).wait(), unroll)
        xb_ref[...] = xrow_ref[slot].astype(BF16)
        acc_ref[slot] = jnp.zeros((tm, acc_ref.shape[2]), F32)

    @pl.when(active)
    def _():
        for u in range(rows_per_step):
            gather_copy(tokn_ref, j * rows_per_step + u, 1 - slot).start()
        xb = xb_ref[...]
        mid = (_silu(_dot(xb, wg_ref[0])) * _dot(xb, wu_ref[0])).astype(BF16)
        acc_ref[slot] += _dot(mid, wd_ref[0])

    @pl.when(active & (j == nj - 1))
    def _():
        @pl.when(i > 0)
        def _():
            for_rows(n_valid_prev, lambda r: scatter_copy(dstp_ref, r, 1 - slot).wait(), unroll)

        for_rows(n_valid, lambda r: scatter_copy(dst_ref, r, slot).start(), unroll)

        @pl.when(i == n_active - 1)
        def _():
            for_rows(n_valid, lambda r: scatter_copy(dst_ref, r, slot).wait(), unroll)
            for_rows(tm, lambda r: gather_copy(tokn_ref, r, 1 - slot).wait(), unroll)


def moe_experts(xn, tile_expert, n_active, n_valid, tok_of_slot, dst_of_slot, wg, wu, wd, *, tm, tf):
    t, d = xn.shape
    e, _, f = wg.shape
    n_tiles = tok_of_slot.shape[0]
    tf = min(tf, f)
    nj = f // tf
    assert tm % nj == 0, "the row prefetch is spread evenly over the grid steps of a tile"

    def w_in_map(i, j, te, na, nv):
        return (te[i], 0, jnp.where(i < na[0], j, nj - 1))

    def w_dn_map(i, j, te, na, nv):
        return (te[i], jnp.where(i < na[0], j, nj - 1), 0)

    def ids_spec(shift):
        return pl.BlockSpec((1, 1, tm), lambda i, j, te, na, nv: (jnp.clip(i + shift, 0, n_tiles - 1), 0, 0),
                            memory_space=pltpu.SMEM)

    grid_spec = pltpu.PrefetchScalarGridSpec(
        num_scalar_prefetch=3,
        grid=(n_tiles, nj),
        in_specs=[
            ids_spec(0), ids_spec(1), ids_spec(0), ids_spec(-1),
            pl.BlockSpec(memory_space=pl.ANY),
            pl.BlockSpec((1, d, tf), w_in_map),
            pl.BlockSpec((1, d, tf), w_in_map),
            pl.BlockSpec((1, tf, d), w_dn_map),
        ],
        out_specs=pl.BlockSpec(memory_space=pl.ANY),
        scratch_shapes=[
            pltpu.VMEM((2, tm, d), F32),
            pltpu.VMEM((tm, d), BF16),
            pltpu.VMEM((2, tm, d), F32),
            pltpu.SemaphoreType.DMA((2,)),
            pltpu.SemaphoreType.DMA((2,)),
        ],
    )
    return pl.pallas_call(
        functools.partial(_moe_kernel, tm=tm, unroll=8 if tm % 8 == 0 else 1, rows_per_step=tm // nj),
        grid_spec=grid_spec,
        out_shape=jax.ShapeDtypeStruct((2 * t, d), F32),
        compiler_params=_cparams(("arbitrary", "arbitrary")),
        name="moe_experts",
    )(tile_expert, n_active, n_valid, tok_of_slot, tok_of_slot, dst_of_slot, dst_of_slot, xn, wg, wu, wd)


def moe_dispatch_tables(info, *, tm, n_experts):
    t = info.shape[0]
    eidx = info[:, 2:4].astype(jnp.int32).reshape(-1)
    onehot = (eidx[:, None] == jnp.arange(n_experts, dtype=jnp.int32)[None, :]).astype(jnp.int32)
    csum = jnp.cumsum(onehot, axis=0)
    rank = jnp.sum((csum - onehot) * onehot, axis=1)
    counts = csum[-1]
    ntile_e = (counts + tm - 1) // tm
    tile_end = jnp.cumsum(ntile_e)
    tile_start = tile_end - ntile_e
    n_active = tile_end[-1]
    n_tiles = (2 * t) // tm + n_experts
    n_slots = n_tiles * tm
    slot = tile_start[eidx] * tm + rank
    pair = jnp.arange(2 * t, dtype=jnp.int32)
    pair_of_slot = jnp.zeros((n_slots,), jnp.int32).at[slot].set(pair, unique_indices=True)
    tok_of_slot = pair_of_slot // 2
    dst_of_slot = (pair_of_slot % 2) * t + pair_of_slot // 2
    tiles = jnp.arange(n_tiles, dtype=jnp.int32)
    te = jnp.searchsorted(tile_end, jnp.minimum(tiles, n_active - 1), side="right").astype(jnp.int32)
    te = jnp.minimum(te, n_experts - 1)
    n_valid = jnp.clip(counts[te] - (tiles - tile_start[te]) * tm, 0, tm)
    n_valid = jnp.where(tiles < n_active, n_valid, 0).astype(jnp.int32)
    return (te, n_active.reshape(1).astype(jnp.int32), n_valid, tok_of_slot.reshape(n_tiles, 1, tm),
            dst_of_slot.reshape(n_tiles, 1, tm))


def _final_kernel(h_ref, y0_ref, y1_ref, info_ref, g_ref, o_ref):
    info = info_ref[...]
    x = h_ref[...] + info[:, 0:1] * y0_ref[...] + info[:, 1:2] * y1_ref[...]
    ms = jnp.mean(x * x, axis=-1, keepdims=True)
    o_ref[...] = x * lax.rsqrt(ms + EPS) * g_ref[...]


def final_combine(h, y, info, gain, *, tm=512):
    t, d = h.shape
    tm = min(tm, t)
    nt = t // tm
    return pl.pallas_call(
        _final_kernel,
        grid=(nt,),
        in_specs=[
            pl.BlockSpec((tm, d), lambda i: (i, 0)),
            pl.BlockSpec((tm, d), lambda i: (i, 0)),
            pl.BlockSpec((tm, d), lambda i: (nt + i, 0)),
            pl.BlockSpec((tm, LANES), lambda i: (i, 0)),
            pl.BlockSpec((1, d), lambda i: (0, 0)),
        ],
        out_specs=pl.BlockSpec((tm, d), lambda i: (i, 0)),
        out_shape=jax.ShapeDtypeStruct((t, d), F32),
        compiler_params=_cparams(("parallel",)),
        name="final_combine",
    )(h, y, y, info, gain.reshape(1, d).astype(F32))


def _softplus(x):
    return jnp.maximum(x, 0.0) + jnp.log1p(jnp.exp(-jnp.abs(x)))


def _gates_kernel(g_ref, p0_ref, p1_ref, col_ref, row_ref, *, mode, n_heads, rows, row_lo):
    x = g_ref[...]
    lane = lax.broadcasted_iota(jnp.int32, x.shape, 1)
    if mode == "gdn":
        logdec = -jnp.exp(p0_ref[...]) * _softplus(x + p1_ref[...])
        second = jax.nn.sigmoid(x)
    else:
        logdec = -_softplus(-(x + p0_ref[...]))
        second = x + p1_ref[...]
    logdec = jnp.where(lane < 2 * n_heads, logdec, 0.0)
    ri = lax.broadcasted_iota(jnp.int32, (CHUNK, CHUNK), 0)
    ci = lax.broadcasted_iota(jnp.int32, (CHUNK, CHUNK), 1)
    lower = (ri >= ci).astype(F32)
    upper = (ri <= ci).astype(F32)
    lane_c = lax.broadcasted_iota(jnp.int32, (CHUNK, LANES), 1)
    fwd_lane = lane_c < n_heads
    for c in range(rows // CHUNK):
        sl = slice(c * CHUNK, (c + 1) * CHUNK)
        ld = logdec[sl]
        cum = jnp.where(fwd_lane, _dot_hi(lower, ld), _dot_hi(upper, ld))
        sec = second[sl]
        if mode == "mlstm":
            sec = pltpu.roll(cum, 2 * n_heads, axis=1) - sec
        out = jnp.where(lane_c < 2 * n_heads, cum, jnp.where(lane_c < 4 * n_heads, sec, 0.0))
        col_ref[sl, :] = out
        row_ref[c] = jnp.transpose(out)[row_lo:row_lo + 2 * n_heads, :]


def gate_prep(g, p0, p1, *, mode, n_heads, rows=512):
    t = g.shape[0]
    rows = min(rows, t)
    assert t % rows == 0 and rows % CHUNK == 0
    row_lo = 0 if mode == "gdn" else 2 * n_heads
    return pl.pallas_call(
        functools.partial(_gates_kernel, mode=mode, n_heads=n_heads, rows=rows, row_lo=row_lo),
        grid=(t // rows,),
        in_specs=[
            pl.BlockSpec((rows, LANES), lambda i: (i, 0)),
            pl.BlockSpec((1, LANES), lambda i: (0, 0)),
            pl.BlockSpec((1, LANES), lambda i: (0, 0)),
        ],
        out_specs=[
            pl.BlockSpec((rows, LANES), lambda i: (i, 0)),
            pl.BlockSpec((rows // CHUNK, 2 * n_heads, CHUNK), lambda i: (i, 0, 0)),
        ],
        out_shape=[jax.ShapeDtypeStruct((t, LANES), F32), jax.ShapeDtypeStruct((t // CHUNK, 2 * n_heads, CHUNK), F32)],
        compiler_params=_cparams(("parallel",)),
        name="gate_prep_" + mode,
    )(g, p0, p1)


def _lane_vec(*parts):
    flat = jnp.concatenate([p.reshape(-1).astype(F32) for p in parts])
    return jnp.zeros((1, LANES), F32).at[0, :flat.shape[0]].set(flat)


def _col(tile, lane, j):
    return jnp.sum(jnp.where(lane == j, tile, 0.0), axis=1, keepdims=True)


def _tri_masks(direction):
    ri = lax.broadcasted_iota(jnp.int32, (CHUNK, CHUNK), 0)
    ci = lax.broadcasted_iota(jnp.int32, (CHUNK, CHUNK), 1)
    if direction == 0:
        return ri >= ci, ri > ci
    return ri <= ci, ri < ci


def _unit_tri_inverses(ms):
    ri = lax.broadcasted_iota(jnp.int32, (CHUNK, CHUNK), 0)
    ci = lax.broadcasted_iota(jnp.int32, (CHUNK, CHUNK), 1)
    eye = jnp.where(ri == ci, 1.0, 0.0)
    blk = ri // SUB == ci // SUB
    ds = [jnp.where(blk, m, 0.0) for m in ms]
    offs = [m - d for m, d in zip(ms, ds)]
    ps = [eye - d for d in ds]
    dps = ds
    for _ in range(3):
        dps = [_mm16(dp, dp) for dp in dps]
        ps = [p + _mm16(p, dp) for p, dp in zip(ps, dps)]
    ns = [_mm16(p, off) for p, off in zip(ps, offs)]
    n2s = [_mm16(n, n) for n in ns]
    qs = [(eye - n) + _mm16(eye - n, n2) for n, n2 in zip(ns, n2s)]
    return [_mm16(q, p) for q, p in zip(qs, ps)]


def _gdn_kernel(q_ref, k_ref, v_ref, z_ref, cwq_ref, cwk_ref, cwv_ref, gc_ref, gr_ref, gn_ref, o_ref,
                pad_ref, qn_ref, kn_ref, vn_ref, aq_ref, b_ref, egl_ref, oacc_ref,
                *, seq, n_heads, conv_rows, dk, group):
    h = pl.program_id(1)
    n_chunks = seq // CHUNK
    half = GDN_CONV // 2

    for src, cw_ref, dst, kind in ((q_ref, cwq_ref, qn_ref, "q"), (k_ref, cwk_ref, kn_ref, "k"), (v_ref, cwv_ref, vn_ref, "v")):
        pad_ref[0:8, :] = jnp.zeros((8, LANES), F32)
        pad_ref[seq + 8:seq + 16, :] = jnp.zeros((8, LANES), F32)
        pad_ref[8:seq + 8, :] = src[0].astype(F32)
        w = cw_ref[...]

        def conv_block(i, c, w=w, dst=dst, kind=kind):
            t0 = pl.multiple_of(i * conv_rows, conv_rows)
            win = pad_ref[pl.ds(t0, conv_rows + 16), :]
            acc = jnp.zeros((conv_rows, LANES), F32)
            for j in range(GDN_CONV):
                off = 8 - half + j
                acc = acc + win[off:off + conv_rows, :] * w[j:j + 1, :]
            y = _silu(acc)
            if kind != "v":
                y = y * lax.rsqrt(jnp.sum(y * y, axis=-1, keepdims=True) + EPS)
            if kind == "q":
                y = y * (dk ** -0.5)
            dst[pl.ds(t0, conv_rows), :] = y
            return c

        lax.fori_loop(0, seq // conv_rows, conv_block, 0)

    oacc_ref[...] = jnp.zeros_like(oacc_ref)
    lane = lax.broadcasted_iota(jnp.int32, (CHUNK, LANES), 1)

    def prepare(g, carry):
        chunks = []
        for j in range(group):
            c = g * group + j
            r0 = pl.multiple_of(c * CHUNK, CHUNK)
            qc = qn_ref[pl.ds(r0, CHUNK), :]
            kc = kn_ref[pl.ds(r0, CHUNK), :]
            chunks.append(dict(c=c, r0=r0, tile=gc_ref[0, pl.ds(r0, CHUNK), :], qc=qc, kc=kc,
                               vc=vn_ref[pl.ds(r0, CHUNK), :], q16=qc.astype(BF16), k16=kc.astype(BF16)))
        for ch in chunks:
            ch["kk"] = _dot_nt(ch["k16"], ch["k16"])
        for ch in chunks:
            ch["qk"] = _dot_nt(ch["q16"], ch["k16"])
        chains = []
        for ch in chunks:
            kt = jnp.transpose(ch["kc"])
            for direction in range(2):
                jd = direction * n_heads + h
                incl, strict = _tri_masks(direction)
                gcol = _col(ch["tile"], lane, jd)
                bcol = _col(ch["tile"], lane, 2 * n_heads + jd)
                grow = gr_ref[0, ch["c"], pl.ds(jd, 1), :]
                glast = gcol[CHUNK - 1:CHUNK] if direction == 0 else gcol[0:1]
                gam = jnp.exp(jnp.where(incl, gcol - grow, -jnp.inf))
                eg = jnp.exp(gcol)
                rhs = jnp.concatenate([ch["vc"] * bcol, ch["kc"] * (bcol * eg)], axis=1)
                rhs_hi = rhs.astype(BF16)
                chains.append(dict(
                    ch=ch, direction=direction, m=jnp.where(strict, ch["kk"] * bcol * gam, 0.0),
                    rhs_hi=rhs_hi, rhs_lo=(rhs - rhs_hi.astype(F32)).astype(BF16),
                    lhs=jnp.concatenate([(kt * jnp.exp(glast - grow)).astype(BF16), (ch["qk"] * gam).astype(BF16)], axis=0),
                    qd=ch["qc"] * eg, egl=jnp.exp(glast)))
        tinvs = [t.astype(BF16) for t in _unit_tri_inverses([cn["m"] for cn in chains])]
        sols = [_dot(t, cn["rhs_hi"]) + _dot(t, cn["rhs_lo"]) for t, cn in zip(tinvs, chains)]
        for cn, sol in zip(chains, sols):
            uw = jnp.concatenate([sol[:, :LANES], -sol[:, LANES:]], axis=1).astype(BF16)
            cn["r"] = _dot(cn["lhs"], uw)
        for cn in chains:
            d, c, r = cn["direction"], cn["ch"]["c"], cn["r"]
            aq_ref[d, c, 0:dk, :] = r[:dk, LANES:].astype(BF16)
            aq_ref[d, c, dk:dk + CHUNK, :] = (cn["qd"] + r[dk:, LANES:]).astype(BF16)
            b_ref[d, c] = r[:dk, :LANES].astype(BF16)
            egl_ref[d, c] = jnp.broadcast_to(cn["egl"], (8, LANES))
            oacc_ref[pl.ds(cn["ch"]["r0"], CHUNK), :] += r[dk:, :LANES]
        return carry

    lax.fori_loop(0, n_chunks // group, prepare, 0)

    def scan_step(c, direction, state):
        r0 = pl.multiple_of(c * CHUNK, CHUNK)
        r = _dot(aq_ref[direction, c], state.astype(BF16))
        oacc_ref[pl.ds(r0, CHUNK), :] += r[dk:]
        return state * egl_ref[direction, c][0:1, :] + r[:dk] + b_ref[direction, c].astype(F32)

    def scan(i, carry):
        sf, sb = carry
        return scan_step(i, 0, sf), scan_step(n_chunks - 1 - i, 1, sb)

    zero = jnp.zeros((dk, LANES), F32)
    lax.fori_loop(0, n_chunks, scan, (zero, zero))

    def epilogue(i, c):
        t0 = pl.multiple_of(i * conv_rows, conv_rows)
        o = oacc_ref[pl.ds(t0, conv_rows), :]
        ms = jnp.mean(o * o, axis=-1, keepdims=True)
        y = o * lax.rsqrt(ms + EPS) * gn_ref[...]
        z = z_ref[0, pl.ds(t0, conv_rows), :].astype(F32)
        o_ref[0, pl.ds(t0, conv_rows), :] = (y * _silu(z)).astype(o_ref.dtype)
        return c

    lax.fori_loop(0, seq // conv_rows, epilogue, 0)


def gdn_mixer(proj, conv_w, gcol, grow, gdn_norm, *, batch, seq, n_heads, col0, conv_rows=256, group=8):
    hb = n_heads
    conv_rows = min(conv_rows, seq)
    n_chunks = seq // CHUNK
    dk = LANES
    group = math.gcd(group, n_chunks)

    def head_spec(group):
        return pl.BlockSpec((1, seq, LANES), lambda b, h, group=group: (b, 0, col0 + group * hb + h))

    def conv_spec(group):
        return pl.BlockSpec((GDN_CONV, LANES), lambda b, h, group=group: (0, group * hb + h))

    return pl.pallas_call(
        functools.partial(_gdn_kernel, seq=seq, n_heads=n_heads, conv_rows=conv_rows, dk=dk, group=group),
        grid=(batch, n_heads),
        in_specs=[
            head_spec(0), head_spec(1), head_spec(2), head_spec(3),
            conv_spec(0), conv_spec(1), conv_spec(2),
            pl.BlockSpec((1, seq, LANES), lambda b, h: (b, 0, 0)),
            pl.BlockSpec((1, n_chunks, 2 * n_heads, CHUNK), lambda b, h: (b, 0, 0, 0)),
            pl.BlockSpec((1, LANES), lambda b, h: (0, 0)),
        ],
        out_specs=pl.BlockSpec((1, seq, LANES), lambda b, h: (b, 0, h)),
        out_shape=jax.ShapeDtypeStruct((batch, seq, n_heads * LANES), BF16),
        scratch_shapes=[
            pltpu.VMEM((seq + 16, LANES), F32),
            pltpu.VMEM((seq, LANES), F32),
            pltpu.VMEM((seq, LANES), F32),
            pltpu.VMEM((seq, LANES), F32),
            pltpu.VMEM((2, n_chunks, dk + CHUNK, LANES), BF16),
            pltpu.VMEM((2, n_chunks, dk, LANES), BF16),
            pltpu.VMEM((2, n_chunks, 8, LANES), F32),
            pltpu.VMEM((seq, LANES), F32),
        ],
        compiler_params=_cparams(("parallel", "arbitrary")),
        name="gdn_mixer",
    )(proj, proj, proj, proj, conv_w, conv_w, conv_w,
      gcol.reshape(batch, seq, LANES), grow.reshape(batch, n_chunks, 2 * n_heads, CHUNK),
      gdn_norm.reshape(1, LANES).astype(F32))


def _mlstm_kernel(q_ref, k_ref, v_ref, og_ref, gc_ref, gr_ref, nrm_ref, o_ref,
                  hf_ref, hb_ref, st_ref, *, seq, n_heads, rows, dqk, dv):
    p = pl.program_id(1)
    n_chunks = seq // CHUNK
    lane = lax.broadcasted_iota(jnp.int32, (CHUNK, LANES), 1)
    ones = jnp.ones((CHUNK, dv), BF16)
    scale = dqk ** -0.5

    st_ref[...] = jnp.zeros_like(st_ref)

    def body(i, ms):
        chains = []
        for direction in range(2):
            c = i if direction == 0 else n_chunks - 1 - i
            r0 = pl.multiple_of(c * CHUNK, CHUNK)
            incl, _ = _tri_masks(direction)
            tile = gc_ref[0, pl.ds(r0, CHUNK), :]
            for hh in range(2):
                idx = direction * 2 + hh
                m = ms[idx]
                jd = direction * n_heads + 2 * p + hh
                bc = _col(tile, lane, jd)
                ac = _col(tile, lane, 2 * n_heads + jd)
                arow = gr_ref[0, c, pl.ds(jd, 1), :]
                blast = bc[CHUNK - 1:CHUNK] if direction == 0 else bc[0:1]
                dmat = jnp.where(incl, bc - arow, NEG)
                inter = bc + m
                mj = jnp.maximum(inter, jnp.max(dmat, axis=1, keepdims=True))
                wl = blast - ac
                m_new = jnp.maximum(blast + m, jnp.max(wl, axis=0, keepdims=True))
                k = k_ref[0, pl.ds(r0, CHUNK), hh * dqk:(hh + 1) * dqk]
                chains.append(dict(
                    idx=idx, r0=r0, hh=hh, dst=hf_ref if direction == 0 else hb_ref, mj=mj, m_new=m_new,
                    q=q_ref[0, pl.ds(r0, CHUNK), hh * dqk:(hh + 1) * dqk], k=k,
                    vaug=jnp.concatenate([v_ref[0, pl.ds(r0, CHUNK), hh * dv:(hh + 1) * dv], ones], axis=1),
                    gate=jnp.exp(dmat - mj) * scale, iw=jnp.exp(inter - mj), decay=jnp.exp(blast + m - m_new),
                    kw=(k.astype(F32) * (jnp.exp(wl - m_new) * scale)).astype(BF16), state=st_ref[idx]))
        for cn in chains:
            cn["qk"] = _dot_nt(cn["q"], cn["k"])
        for cn in chains:
            cn["qc"] = _dot(cn["q"], cn["state"].astype(BF16))
        for cn in chains:
            cn["upd"] = _dot_tn(cn["kw"], cn["vaug"])
        for cn in chains:
            cn["sv"] = _dot((cn["qk"] * cn["gate"]).astype(BF16), cn["vaug"])
        for cn in chains:
            num = cn["iw"] * cn["qc"] + cn["sv"]
            den = num[:, dv:dv + 1]
            hout = num[:, :dv] / jnp.maximum(jnp.abs(den), jnp.exp(-cn["mj"]))
            cn["dst"][pl.ds(cn["r0"], CHUNK), cn["hh"] * dv:(cn["hh"] + 1) * dv] = hout
            st_ref[cn["idx"]] = cn["decay"] * cn["state"] + cn["upd"]
        return tuple(cn["m_new"] for cn in chains)

    zero = jnp.zeros((1, 1), F32)
    lax.fori_loop(0, n_chunks, body, (zero, zero, zero, zero))

    def epilogue(i, c):
        t0 = pl.multiple_of(i * rows, rows)
        g = nrm_ref[0]
        og = og_ref[0, pl.ds(t0, rows), :].astype(F32)
        for hh in range(2):
            sl = slice(hh * dv, (hh + 1) * dv)
            x = hf_ref[pl.ds(t0, rows), sl] + hb_ref[pl.ds(t0, rows), sl]
            ms = jnp.mean(x * x, axis=-1, keepdims=True)
            y = x * lax.rsqrt(ms + EPS) * g[:, sl]
            o_ref[0, pl.ds(t0, rows), sl] = (y * jax.nn.sigmoid(og[:, sl])).astype(o_ref.dtype)
        return c

    lax.fori_loop(0, seq // rows, epilogue, 0)


def mlstm_mixer(proj, gcol, grow, mlstm_norm, *, batch, seq, n_heads, q_blk, k_blk, v_blk, o_blk, rows=256):
    dqk, dv = MLSTM_DQK, MLSTM_DV
    n_chunks = seq // CHUNK
    rows = min(rows, seq)
    npair = n_heads // 2
    return pl.pallas_call(
        functools.partial(_mlstm_kernel, seq=seq, n_heads=n_heads, rows=rows, dqk=dqk, dv=dv),
        grid=(batch, npair),
        in_specs=[
            pl.BlockSpec((1, seq, 2 * dqk), lambda b, p: (b, 0, q_blk + p)),
            pl.BlockSpec((1, seq, 2 * dqk), lambda b, p: (b, 0, k_blk + p)),
            pl.BlockSpec((1, seq, 2 * dv), lambda b, p: (b, 0, v_blk + p)),
            pl.BlockSpec((1, seq, 2 * dv), lambda b, p: (b, 0, o_blk + p)),
            pl.BlockSpec((1, seq, LANES), lambda b, p: (b, 0, 0)),
            pl.BlockSpec((1, n_chunks, 2 * n_heads, CHUNK), lambda b, p: (b, 0, 0, 0)),
            pl.BlockSpec((1, 1, 2 * dv), lambda b, p: (p, 0, 0)),
        ],
        out_specs=pl.BlockSpec((1, seq, 2 * dv), lambda b, p: (b, 0, p)),
        out_shape=jax.ShapeDtypeStruct((batch, seq, n_heads * dv), BF16),
        scratch_shapes=[
            pltpu.VMEM((seq, 2 * dv), F32),
            pltpu.VMEM((seq, 2 * dv), F32),
            pltpu.VMEM((4, dqk, 2 * dv), F32),
        ],
        compiler_params=_cparams(("parallel", "arbitrary")),
        name="mlstm_mixer",
    )(proj, proj, proj, proj, gcol.reshape(batch, seq, LANES), grow.reshape(batch, n_chunks, 2 * n_heads, CHUNK),
      mlstm_norm.reshape(npair, 1, 2 * dv).astype(F32))


def _dsw_kernel(slope_ref, q_ref, k_ref, v_ref, o_ref, l_ref, *, u_len, dil, half, n_heads, qb, group):
    h = pl.program_id(1) % n_heads
    bias_step = slope_ref[h] * float(dil)
    kb = qb + 2 * half
    scale = LANES ** -0.5

    rel = lax.broadcasted_iota(jnp.int32, (qb, kb), 1) - lax.broadcasted_iota(jnp.int32, (qb, kb), 0)

    def blocks(g, c):
        work = []
        for j in range(group):
            u0 = pl.multiple_of((g * group + j) * qb, qb)
            ks = pl.multiple_of(jnp.clip(u0 - half, 0, u_len - kb), half)
            work.append(dict(u0=u0, dist=jnp.abs(rel + (ks - u0)), q=q_ref[0, pl.ds(u0, qb), :],
                             k=k_ref[0, pl.ds(ks, kb), :], v=v_ref[0, pl.ds(ks, kb), :]))
        for w in work:
            w["s"] = _dot_nt(w["q"], w["k"])
        for w in work:
            s = w["s"] * scale - bias_step * w["dist"].astype(F32)
            s = jnp.where(w["dist"] <= half, s, NEG)
            w["mx"] = jnp.max(s, axis=-1, keepdims=True)
            p = jnp.exp(s - w["mx"])
            w["den"] = jnp.sum(p, axis=-1, keepdims=True)
            w["p"] = p.astype(BF16)
        for w in work:
            w["o"] = _dot(w["p"], w["v"])
        for w in work:
            o_ref[0, pl.ds(w["u0"], qb), :] = w["o"] / w["den"]
            l_ref[0, pl.ds(w["u0"], qb), :] = jnp.broadcast_to(w["mx"] + jnp.log(w["den"]), (qb, LANES))
        return c

    lax.fori_loop(0, u_len // (qb * group), blocks, 0)


def dsw_branch(proj, slopes, *, batch, seq, width, n_heads, q_blk, k_blk, v_blk, window, dil, qb=128):
    u_len = seq // dil
    half = window // (2 * dil)
    wb = width // LANES
    view = proj.reshape(batch, u_len, dil * width)

    def spec(blk):
        return pl.BlockSpec((1, u_len, LANES), lambda b, j, s, blk=blk: (b, 0, (j // n_heads) * wb + blk + j % n_heads))

    out_spec = pl.BlockSpec((1, u_len, LANES), lambda b, j, s: (b, 0, j))
    shape = jax.ShapeDtypeStruct((batch, u_len, dil * n_heads * LANES), F32)
    o, l = pl.pallas_call(
        functools.partial(_dsw_kernel, u_len=u_len, dil=dil, half=half, n_heads=n_heads, qb=qb,
                          group=math.gcd(4, u_len // qb)),
        grid_spec=pltpu.PrefetchScalarGridSpec(
            num_scalar_prefetch=1,
            grid=(batch, dil * n_heads),
            in_specs=[spec(q_blk), spec(k_blk), spec(v_blk)],
            out_specs=[out_spec, out_spec],
        ),
        out_shape=[shape, shape],
        compiler_params=_cparams(("parallel", "arbitrary")),
        name="dsw_branch_r%d" % dil,
    )(slopes, view, view, view)
    return o.reshape(batch * seq, n_heads * LANES), l.reshape(batch * seq, n_heads * LANES)


def _dsw_merge_kernel(o1, l1, o2, l2, o3, l3, out_ref):
    a, b, c = l1[...], l2[...], l3[...]
    mx = jnp.maximum(jnp.maximum(a, b), c)
    ea, eb, ec = jnp.exp(a - mx), jnp.exp(b - mx), jnp.exp(c - mx)
    tot = ea + eb + ec
    out_ref[...] = ((o1[...] * ea + o2[...] * eb + o3[...] * ec) / tot).astype(out_ref.dtype)


def dsw_merge(parts, *, tm=512):
    t, w = parts[0][0].shape
    tm = min(tm, t)
    flat = [a for pair in parts for a in pair]
    spec = pl.BlockSpec((tm, w), lambda i: (i, 0))
    return pl.pallas_call(
        _dsw_merge_kernel,
        grid=(t // tm,),
        in_specs=[spec] * 6,
        out_specs=spec,
        out_shape=jax.ShapeDtypeStruct((t, w), BF16),
        compiler_params=_cparams(("parallel",)),
        name="dsw_merge",
    )(*flat)


def _mla_kernel(q_ref, tq_ref, kn_ref, v_ref, kr_ref, krr_ref, ck_ref, sk_ref, o_ref, k_scr, v_scr, *, seq, kvb):
    @pl.when(pl.program_id(2) == 0)
    def _():
        k_scr[:, :LANES] = kn_ref[0]
        rk = kr_ref[0].astype(F32) * ck_ref[...] + krr_ref[0].astype(F32) * sk_ref[...]
        k_scr[:, LANES:] = rk.astype(BF16)
        v_scr[:, :LANES] = v_ref[0]
        v_scr[:, LANES:] = jnp.ones((seq, LANES), BF16)

    q = (q_ref[0].astype(F32) * tq_ref[...]).astype(BF16)
    mx = jnp.full((q.shape[0], 1), -jnp.inf, F32)
    acc = jnp.zeros((q.shape[0], 2 * LANES), F32)
    for b in range(seq // kvb):
        s = _dot_nt(q, k_scr[b * kvb:(b + 1) * kvb, :])
        mx_new = jnp.maximum(mx, jnp.max(s, axis=-1, keepdims=True))
        p = jnp.exp2(s - mx_new).astype(BF16)
        acc = acc * jnp.exp2(mx - mx_new) + _dot(p, v_scr[b * kvb:(b + 1) * kvb, :])
        mx = mx_new
    o_ref[0] = (acc[:, :LANES] / acc[:, LANES:LANES + 1]).astype(o_ref.dtype)


def mla_attention(q_up, kv_up, proj, tq_tab, ck_tab, sk_tab, *, batch, seq, n_heads, kr_blk, krr_blk, tq=512, kvb=512):
    tq = min(tq, seq)
    kvb = min(kvb, seq)
    return pl.pallas_call(
        functools.partial(_mla_kernel, seq=seq, kvb=kvb),
        grid=(batch, n_heads, seq // tq),
        in_specs=[
            pl.BlockSpec((1, tq, 2 * LANES), lambda b, h, i: (b, i, h)),
            pl.BlockSpec((tq, 2 * LANES), lambda b, h, i: (i, 0)),
            pl.BlockSpec((1, seq, LANES), lambda b, h, i: (b, 0, 2 * h)),
            pl.BlockSpec((1, seq, LANES), lambda b, h, i: (b, 0, 2 * h + 1)),
            pl.BlockSpec((1, seq, LANES), lambda b, h, i: (b, 0, kr_blk)),
            pl.BlockSpec((1, seq, LANES), lambda b, h, i: (b, 0, krr_blk)),
            pl.BlockSpec((seq, LANES), lambda b, h, i: (0, 0)),
            pl.BlockSpec((seq, LANES), lambda b, h, i: (0, 0)),
        ],
        out_specs=pl.BlockSpec((1, tq, LANES), lambda b, h, i: (b, i, h)),
        out_shape=jax.ShapeDtypeStruct((batch, seq, n_heads * LANES), BF16),
        scratch_shapes=[pltpu.VMEM((seq, 2 * LANES), BF16), pltpu.VMEM((seq, 2 * LANES), BF16)],
        compiler_params=_cparams(("parallel", "parallel", "arbitrary")),
        name="mla_attention",
    )(q_up, tq_tab, kv_up, kv_up, proj, proj, ck_tab, sk_tab)


def _rot_cols(w):
    half = w.shape[-1] // 2
    return jnp.concatenate([-w[..., half:], w[..., :half]], axis=-1)


def layer_even(h, batch, seq, mix_norm, w_in, conv_w, a_log, dt_bias, gdn_norm, w_out, ffn_norm, w_gate, w_up, w_down,
               *, tm=512, tm_in=1024):
    d = h.shape[1]
    nh = GDN_HEADS
    qk = nh * LANES
    gate0 = 4 * qk
    att0 = gate0 + 4 * nh
    w_gates = jnp.zeros((d, LANES), F32).at[:, :4 * nh].set(w_in[:, gate0:att0]).astype(BF16)
    proj_a = norm_matmul(h, mix_norm, w_in[:, :gate0].astype(BF16), out_dtype=BF16, tm=tm_in, name="l0_in_proj_gdn")
    proj_b = norm_matmul(h, mix_norm, w_in[:, att0:].astype(BF16), out_dtype=BF16, tm=tm_in, name="l0_in_proj_dsw")
    graw = norm_matmul(h, mix_norm, w_gates, out_dtype=F32, tm=tm_in, name="l0_gate_proj")
    gcol, grow = gate_prep(graw, _lane_vec(a_log), _lane_vec(dt_bias), mode="gdn", n_heads=nh)
    o_a = gdn_mixer(proj_a.reshape(batch, seq, gate0), conv_w.astype(F32), gcol, grow, gdn_norm,
                    batch=batch, seq=seq, n_heads=nh, col0=0)
    slopes = (2.0 ** (-8.0 * jnp.arange(1, DSW_HEADS + 1, dtype=F32) / DSW_HEADS)).astype(F32)
    width_b = proj_b.shape[1]
    proj_b3 = proj_b.reshape(batch, seq, width_b)
    parts = [dsw_branch(proj_b3, slopes, batch=batch, seq=seq, width=width_b, n_heads=DSW_HEADS,
                        q_blk=0, k_blk=DSW_HEADS, v_blk=2 * DSW_HEADS, window=win, dil=r)
             for win, r in DSW_PAIRS]
    o_b = dsw_merge(parts, tm=tm)
    na = nh * LANES
    h = proj_residual(o_a.reshape(batch * seq, na), o_b, w_out[:na].astype(BF16), w_out[na:].astype(BF16), h,
                      tm=tm, name="l0_out_proj")
    return ffn_residual(h, ffn_norm, w_gate.astype(BF16), w_up.astype(BF16), w_down.astype(BF16), tm=tm)


def layer_odd_mixers(h, batch, seq, mix_norm, w_in, q_norm, kv_norm, w_uq, w_ukv, ig_bias, fg_bias, mlstm_norm, w_out,
                     *, tm=512, tm_in=1024):
    d = h.shape[1]
    nh = MLSTM_HEADS
    sizes = (MLA_Q_RANK, MLA_KV_RANK, MLA_ROPE, nh * MLSTM_DQK, nh * MLSTM_DQK, nh * MLSTM_DV, nh * MLSTM_DV, 2 * nh, 2 * nh)
    cuts = np.cumsum((0,) + sizes)
    w_cq, w_ckv, w_kr, w_mq, w_mk, w_mv, w_mo, w_mi, w_mf = (w_in[:, cuts[i]:cuts[i + 1]] for i in range(9))
    w_krot = _rot_cols(w_kr)
    w_main = jnp.concatenate([w_cq, w_kr, w_kr, w_krot, w_krot, w_ckv, w_mq, w_mk, w_mv, w_mo], axis=1).astype(BF16)
    w_gates = jnp.zeros((d, LANES), F32).at[:, :2 * nh].set(w_mf).at[:, 2 * nh:4 * nh].set(w_mi).astype(BF16)
    proj = norm_matmul(h, mix_norm, w_main, out_dtype=BF16, tm=tm_in, name="l1_in_proj")
    graw = norm_matmul(h, mix_norm, w_gates, out_dtype=F32, tm=tm_in, name="l1_gate_proj")
    gcol, grow = gate_prep(graw, _lane_vec(fg_bias), _lane_vec(jnp.zeros((2 * nh,), F32), ig_bias), mode="mlstm", n_heads=nh)
    width = proj.shape[1]
    proj3 = proj.reshape(batch, seq, width)

    hq = MLA_HEADS
    wq = w_uq.reshape(MLA_Q_RANK, hq, MLA_NOPE + MLA_ROPE)
    wq_rope = wq[:, :, MLA_NOPE:]
    wq_all = jnp.concatenate([wq[:, :, :MLA_NOPE], wq_rope, _rot_cols(wq_rope)], axis=-1).reshape(MLA_Q_RANK, hq * 2 * LANES)
    q_up = norm_matmul(proj, q_norm, wq_all.astype(BF16), out_dtype=BF16, k_block=0, k_width=MLA_Q_RANK, tm=tm_in, name="mla_q_up")
    kv_blk = (MLA_Q_RANK + 2 * LANES) // MLA_KV_RANK
    kv_up = norm_matmul(proj, kv_norm, w_ukv.astype(BF16), out_dtype=BF16, k_block=kv_blk, k_width=MLA_KV_RANK, tm=tm_in, name="mla_kv_up")
    pos = jnp.arange(seq, dtype=F32)
    freqs = ROPE_THETA ** (-jnp.arange(0, MLA_ROPE, 2, dtype=F32) / MLA_ROPE)
    ang = pos[:, None] * freqs[None, :]
    cos, sin = jnp.cos(ang), jnp.sin(ang)
    scale = LOG2E * (MLA_NOPE + MLA_ROPE) ** -0.5
    tq_tab = scale * jnp.concatenate([jnp.ones((seq, MLA_NOPE), F32), cos, cos, sin, sin], axis=1)
    ck_tab = jnp.concatenate([cos] * 4, axis=1)
    sk_tab = jnp.concatenate([sin] * 4, axis=1)
    kr_blk = MLA_Q_RANK // LANES
    o_c = mla_attention(q_up.reshape(batch, seq, -1), kv_up.reshape(batch, seq, -1), proj3, tq_tab, ck_tab, sk_tab,
                        batch=batch, seq=seq, n_heads=hq, kr_blk=kr_blk, krr_blk=kr_blk + 1)

    mq0 = kr_blk + 2 + MLA_KV_RANK // LANES
    mk0 = mq0 + nh * MLSTM_DQK // LANES
    mv0 = mk0 + nh * MLSTM_DQK // LANES
    mo0 = mv0 + nh * MLSTM_DV // LANES
    o_d = mlstm_mixer(proj3, gcol, grow, mlstm_norm, batch=batch, seq=seq, n_heads=nh,
                      q_blk=mq0, k_blk=mk0, v_blk=mv0 // 2, o_blk=mo0 // 2)
    nc = hq * MLA_V
    return proj_residual(o_c.reshape(batch * seq, nc), o_d.reshape(batch * seq, -1), w_out[:nc].astype(BF16),
                         w_out[nc:].astype(BF16), h, tm=tm, name="l1_out_proj")


def moe_block(h, ffn_norm, w_router, b_router, we_gate, we_up, we_down, final_norm, *, tm=448, tf=512, tm_tok=512):
    t = h.shape[0]
    n_experts = we_gate.shape[0]
    tm = min(tm, t)
    xn, info = router(h, ffn_norm, w_router, b_router, tm=tm_tok)
    te, n_active, n_valid, tok_of_slot, dst_of_slot = moe_dispatch_tables(info, tm=tm, n_experts=n_experts)
    y = moe_experts(xn, te, n_active, n_valid, tok_of_slot, dst_of_slot, we_gate.astype(BF16), we_up.astype(BF16),
                    we_down.astype(BF16), tm=tm, tf=tf)
    return final_combine(h, y, info, final_norm, tm=tm_tok)


def kernel(x, even_mix_norm, even_w_in, even_conv_w, even_a_log, even_dt_bias, even_gdn_norm, even_w_out,
           even_ffn_norm, even_w_gate, even_w_up, even_w_down, odd_mix_norm, odd_w_in, odd_q_norm, odd_kv_norm,
           odd_w_uq, odd_w_ukv, odd_ig_bias, odd_fg_bias, odd_mlstm_norm, odd_w_out, odd_ffn_norm,
           odd_w_router, odd_b_router, odd_we_gate, odd_we_up, odd_we_down, final_norm):
    batch, seq, d = x.shape
    h = x.reshape(batch * seq, d)
    h = layer_even(h, batch, seq, even_mix_norm[0], even_w_in[0], even_conv_w[0], even_a_log[0], even_dt_bias[0],
                   even_gdn_norm[0], even_w_out[0], even_ffn_norm[0], even_w_gate[0], even_w_up[0], even_w_down[0])
    h = layer_odd_mixers(h, batch, seq, odd_mix_norm[0], odd_w_in[0], odd_q_norm[0], odd_kv_norm[0], odd_w_uq[0],
                         odd_w_ukv[0], odd_ig_bias[0], odd_fg_bias[0], odd_mlstm_norm[0], odd_w_out[0])
    out = moe_block(h, odd_ffn_norm[0], odd_w_router[0], odd_b_router[0], odd_we_gate[0], odd_we_up[0],
                    odd_we_down[0], final_norm)
    return out.reshape(batch, seq, d)
```

```python
import functools
import math

import jax
import jax.numpy as jnp
import numpy as np
from jax import lax
from jax.experimental import pallas as pl
from jax.experimental.pallas import tpu as pltpu

F32 = jnp.float32
BF16 = jnp.bfloat16
EPS = 1e-6
NEG = -1e30
HI = lax.Precision.HIGHEST
LOG2E = math.log2(math.e)

LANES = 128
CHUNK = 64
SUB = 16
VMEM_LIMIT = 56 * 1024 * 1024

GDN_HEADS = 8
GDN_CONV = 5
DSW_HEADS = 8
DSW_PAIRS = ((128, 1), (512, 4), (2048, 16))
MLA_HEADS = 8
MLA_Q_RANK = 768
MLA_KV_RANK = 512
MLA_NOPE = 128
MLA_ROPE = 64
MLA_V = 128
ROPE_THETA = 10000.0
MLSTM_HEADS = 8
MLSTM_DQK = 64
MLSTM_DV = 128


def _cparams(sem):
    return pltpu.CompilerParams(dimension_semantics=sem, vmem_limit_bytes=VMEM_LIMIT)


def _dot(a, b):
    return jnp.dot(a, b, preferred_element_type=F32)


def _dot_nt(a, b):
    return lax.dot_general(a, b, (((1,), (1,)), ((), ())), preferred_element_type=F32)


def _dot_tn(a, b):
    return lax.dot_general(a, b, (((0,), (0,)), ((), ())), preferred_element_type=F32)


def _dot_hi(a, b):
    return jnp.dot(a, b, preferred_element_type=F32, precision=HI)


def _mm16(a, b):
    return _dot(a.astype(BF16), b.astype(BF16))


def _nm_kernel(x_ref, g_ref, w_ref, o_ref, xn_ref, *, use_norm):
    @pl.when(pl.program_id(1) == 0)
    def _():
        x = x_ref[...].astype(F32)
        if use_norm:
            ms = jnp.mean(x * x, axis=-1, keepdims=True)
            x = x * lax.rsqrt(ms + EPS) * g_ref[...]
        xn_ref[...] = x.astype(BF16)

    o_ref[...] = _dot(xn_ref[...], w_ref[...]).astype(o_ref.dtype)


def norm_matmul(x, gain, w, *, out_dtype, k_block=0, k_width=None, tm=512, tn=512, use_norm=True, name="norm_matmul"):
    t = x.shape[0]
    kw = x.shape[1] if k_width is None else k_width
    n = w.shape[1]
    tm = min(tm, t)
    tn = min(tn, n)
    assert t % tm == 0 and n % tn == 0 and w.shape[0] == kw
    g2 = gain.reshape(1, kw).astype(F32)
    return pl.pallas_call(
        functools.partial(_nm_kernel, use_norm=use_norm),
        grid=(t // tm, n // tn),
        in_specs=[
            pl.BlockSpec((tm, kw), lambda i, j: (i, k_block)),
            pl.BlockSpec((1, kw), lambda i, j: (0, 0)),
            pl.BlockSpec((kw, tn), lambda i, j: (0, j)),
        ],
        out_specs=pl.BlockSpec((tm, tn), lambda i, j: (i, j)),
        out_shape=jax.ShapeDtypeStruct((t, n), out_dtype),
        scratch_shapes=[pltpu.VMEM((tm, kw), BF16)],
        compiler_params=_cparams(("parallel", "arbitrary")),
        name=name,
    )(x, g2, w)


def _proj_res_kernel(a_ref, b_ref, w1_ref, w2_ref, r_ref, o_ref):
    acc = _dot(a_ref[...], w1_ref[...]) + _dot(b_ref[...], w2_ref[...])
    o_ref[...] = r_ref[...] + acc


def proj_residual(a, b, w1, w2, res, *, tm=512, name="proj_residual"):
    t, ka = a.shape
    kb = b.shape[1]
    n = w1.shape[1]
    tm = min(tm, t)
    assert t % tm == 0
    return pl.pallas_call(
        _proj_res_kernel,
        grid=(t // tm,),
        in_specs=[
            pl.BlockSpec((tm, ka), lambda i: (i, 0)),
            pl.BlockSpec((tm, kb), lambda i: (i, 0)),
            pl.BlockSpec((ka, n), lambda i: (0, 0)),
            pl.BlockSpec((kb, n), lambda i: (0, 0)),
            pl.BlockSpec((tm, n), lambda i: (i, 0)),
        ],
        out_specs=pl.BlockSpec((tm, n), lambda i: (i, 0)),
        out_shape=jax.ShapeDtypeStruct((t, n), F32),
        compiler_params=_cparams(("parallel",)),
        name=name,
    )(a, b, w1, w2, res)


def _silu(a):
    return a * jax.nn.sigmoid(a)


def _ffn_kernel(h_ref, g_ref, wg_ref, wu_ref, wd_ref, o_ref, xn_ref, acc_ref):
    j = pl.program_id(1)

    @pl.when(j == 0)
    def _():
        x = h_ref[...]
        ms = jnp.mean(x * x, axis=-1, keepdims=True)
        xn_ref[...] = (x * lax.rsqrt(ms + EPS) * g_ref[...]).astype(BF16)
        acc_ref[...] = jnp.zeros_like(acc_ref)

    xn = xn_ref[...]
    mid = (_silu(_dot(xn, wg_ref[0])) * _dot(xn, wu_ref[0])).astype(BF16)
    acc_ref[...] += _dot(mid, wd_ref[...])

    @pl.when(j == pl.num_programs(1) - 1)
    def _():
        o_ref[...] = h_ref[...] + acc_ref[...]


def ffn_residual(h, gain, wg, wu, wd, *, tm=512, name="ffn_swiglu"):
    t, d = h.shape
    nj, _, tf = wg.shape
    tm = min(tm, t)
    assert t % tm == 0
    return pl.pallas_call(
        _ffn_kernel,
        grid=(t // tm, nj),
        in_specs=[
            pl.BlockSpec((tm, d), lambda i, j: (i, 0)),
            pl.BlockSpec((1, d), lambda i, j: (0, 0)),
            pl.BlockSpec((1, d, tf), lambda i, j: (j, 0, 0)),
            pl.BlockSpec((1, d, tf), lambda i, j: (j, 0, 0)),
            pl.BlockSpec((tf, d), lambda i, j: (j, 0)),
        ],
        out_specs=pl.BlockSpec((tm, d), lambda i, j: (i, 0)),
        out_shape=jax.ShapeDtypeStruct((t, d), F32),
        scratch_shapes=[pltpu.VMEM((tm, d), BF16), pltpu.VMEM((tm, d), F32)],
        compiler_params=_cparams(("parallel", "arbitrary")),
        name=name,
    )(h, gain.reshape(1, d).astype(F32), wg, wu, wd)


def _router_kernel(h_ref, g_ref, wr_ref, br_ref, xn_ref, info_ref, *, n_experts):
    x = h_ref[...]
    ms = jnp.mean(x * x, axis=-1, keepdims=True)
    xn = x * lax.rsqrt(ms + EPS) * g_ref[...]
    bits = pltpu.bitcast(xn.astype(BF16).astype(F32), jnp.uint32)
    half = bits.shape[1] // 2
    xn_ref[...] = (bits[:, :half] >> 16) | (bits[:, half:] & jnp.uint32(0xFFFF0000))
    logits = _dot_hi(xn, wr_ref[...]) + br_ref[...]
    lane = lax.broadcasted_iota(jnp.int32, logits.shape, 1)
    real = lane < n_experts
    lg = jnp.where(real, logits, -jnp.inf)
    mx = jnp.max(lg, axis=-1, keepdims=True)
    ex = jnp.exp(lg - mx)
    probs = ex / jnp.sum(ex, axis=-1, keepdims=True)
    probs = jnp.where(real, probs, -1.0)
    v1 = jnp.max(probs, axis=-1, keepdims=True)
    i1 = jnp.min(jnp.where(probs == v1, lane, LANES), axis=-1, keepdims=True)
    rest = jnp.where(lane == i1, -1.0, probs)
    v2 = jnp.max(rest, axis=-1, keepdims=True)
    i2 = jnp.min(jnp.where(rest == v2, lane, LANES), axis=-1, keepdims=True)
    tot = v1 + v2
    info = jnp.where(lane == 0, v1 / tot, 0.0)
    info = jnp.where(lane == 1, v2 / tot, info)
    info = jnp.where(lane == 2, i1.astype(F32), info)
    info = jnp.where(lane == 3, i2.astype(F32), info)
    info_ref[...] = info


def router(h, gain, w_router, b_router, *, tm=512):
    t, d = h.shape
    e = w_router.shape[1]
    tm = min(tm, t)
    wr = jnp.zeros((d, LANES), F32).at[:, :e].set(w_router.astype(F32))
    br = jnp.zeros((1, LANES), F32).at[0, :e].set(b_router.astype(F32))
    return pl.pallas_call(
        functools.partial(_router_kernel, n_experts=e),
        grid=(t // tm,),
        in_specs=[
            pl.BlockSpec((tm, d), lambda i: (i, 0)),
            pl.BlockSpec((1, d), lambda i: (0, 0)),
            pl.BlockSpec((d, LANES), lambda i: (0, 0)),
            pl.BlockSpec((1, LANES), lambda i: (0, 0)),
        ],
        out_specs=[pl.BlockSpec((tm, d // 2), lambda i: (i, 0)), pl.BlockSpec((tm, LANES), lambda i: (i, 0))],
        out_shape=[jax.ShapeDtypeStruct((t, d // 2), jnp.uint32), jax.ShapeDtypeStruct((t, LANES), F32)],
        compiler_params=_cparams(("parallel",)),
        name="router",
    )(h, gain.reshape(1, d).astype(F32), wr, br)


def _moe_kernel(te_ref, na_ref, nv_ref, tok_ref, tokn_ref, dst_ref, dstp_ref, xn_hbm, wg_ref, wu_ref, wd_ref, out_hbm,
                xrow_ref, xb_ref, acc_ref, sem_in, sem_out, *, tm, unroll, rows_per_step):
    i = pl.program_id(0)
    j = pl.program_id(1)
    nj = pl.num_programs(1)
    n_active = na_ref[0]
    active = i < n_active
    slot = i % 2
    n_valid = nv_ref[i]
    n_valid_prev = nv_ref[jnp.maximum(i - 1, 0)]

    def gather_copy(ids_ref, r, s):
        return pltpu.make_async_copy(xn_hbm.at[pl.ds(ids_ref[0, 0, r], 1)], xrow_ref.at[s, pl.ds(r, 1)], sem_in.at[s])

    def scatter_copy(ids_ref, r, s):
        return pltpu.make_async_copy(acc_ref.at[s, pl.ds(r, 1)], out_hbm.at[pl.ds(ids_ref[0, 0, r], 1)], sem_out.at[s])

    def for_rows(n, fn, unroll=1):
        def body(r, c):
            fn(r)
            return c

        if isinstance(n, int):
            lax.fori_loop(0, n, body, 0, unroll=unroll)
            return

        def body_blk(b, c):
            for u in range(unroll):
                fn(b * unroll + u)
            return c

        n_blk = n // unroll
        lax.fori_loop(0, n_blk, body_blk, 0)
        lax.fori_loop(n_blk * unroll, n, body, 0)

    @pl.when(active & (j == 0))
    def _():
        @pl.when(i == 0)
        def _():
            for_rows(tm, lambda r: gather_copy(tok_ref, r, 0).start(), unroll)

        for_rows(tm, lambda r: gather_copy(tok_ref, r, slot).wait(), unroll)
        words = xrow_ref[slot]
        half = words.shape[1]
        xb_ref[:, :half] = pltpu.bitcast(words << 16, F32).astype(BF16)
        xb_ref[:, half:] = pltpu.bitcast(words & jnp.uint32(0xFFFF0000), F32).astype(BF16)
        acc_ref[slot] = jnp.zeros((tm, acc_ref.shape[2]), F32)

    @pl.when(active)
    def _():
        for u in range(rows_per_step):
            gather_copy(tokn_ref, j * rows_per_step + u, 1 - slot).start()
        xb = xb_ref[...]
        mid = (_silu(_dot(xb, wg_ref[0, 0])) * _dot(xb, wu_ref[0, 0])).astype(BF16)
        acc_ref[slot] += _dot(mid, wd_ref[0])

    @pl.when(active & (j == nj - 1))
    def _():
        @pl.when(i > 0)
        def _():
            for_rows(n_valid_prev, lambda r: scatter_copy(dstp_ref, r, 1 - slot).wait(), unroll)

        for_rows(n_valid, lambda r: scatter_copy(dst_ref, r, slot).start(), unroll)

        @pl.when(i == n_active - 1)
        def _():
            for_rows(n_valid, lambda r: scatter_copy(dst_ref, r, slot).wait(), unroll)
            for_rows(tm, lambda r: gather_copy(tokn_ref, r, 1 - slot).wait(), unroll)


def moe_experts(xn, tile_expert, n_active, n_valid, tok_of_slot, dst_of_slot, wg, wu, wd, *, tm):
    t = xn.shape[0]
    d = 2 * xn.shape[1]
    e, nj, _, tf = wg.shape
    n_tiles = tok_of_slot.shape[0]
    assert tm % nj == 0, "the row prefetch is spread evenly over the grid steps of a tile"

    def w_in_map(i, j, te, na, nv):
        return (te[i], jnp.where(i < na[0], j, nj - 1), 0, 0)

    def w_dn_map(i, j, te, na, nv):
        return (te[i], jnp.where(i < na[0], j, nj - 1), 0)

    def ids_spec(shift):
        return pl.BlockSpec((1, 1, tm), lambda i, j, te, na, nv: (jnp.clip(i + shift, 0, n_tiles - 1), 0, 0),
                            memory_space=pltpu.SMEM)

    grid_spec = pltpu.PrefetchScalarGridSpec(
        num_scalar_prefetch=3,
        grid=(n_tiles, nj),
        in_specs=[
            ids_spec(0), ids_spec(1), ids_spec(0), ids_spec(-1),
            pl.BlockSpec(memory_space=pl.ANY),
            pl.BlockSpec((1, 1, d, tf), w_in_map),
            pl.BlockSpec((1, 1, d, tf), w_in_map),
            pl.BlockSpec((1, tf, d), w_dn_map),
        ],
        out_specs=pl.BlockSpec(memory_space=pl.ANY),
        scratch_shapes=[
            pltpu.VMEM((2, tm, d // 2), jnp.uint32),
            pltpu.VMEM((tm, d), BF16),
            pltpu.VMEM((2, tm, d), F32),
            pltpu.SemaphoreType.DMA((2,)),
            pltpu.SemaphoreType.DMA((2,)),
        ],
    )
    return pl.pallas_call(
        functools.partial(_moe_kernel, tm=tm, unroll=8 if tm % 8 == 0 else 1, rows_per_step=tm // nj),
        grid_spec=grid_spec,
        out_shape=jax.ShapeDtypeStruct((2 * t, d), F32),
        compiler_params=_cparams(("arbitrary", "arbitrary")),
        name="moe_experts",
    )(tile_expert, n_active, n_valid, tok_of_slot, tok_of_slot, dst_of_slot, dst_of_slot, xn, wg, wu, wd)


def moe_dispatch_tables(info, *, tm, n_experts):
    t = info.shape[0]
    eidx = info[:, 2:4].astype(jnp.int32).reshape(-1)
    onehot = (eidx[:, None] == jnp.arange(n_experts, dtype=jnp.int32)[None, :]).astype(jnp.int32)
    csum = jnp.cumsum(onehot, axis=0)
    rank = jnp.sum((csum - onehot) * onehot, axis=1)
    counts = csum[-1]
    ntile_e = (counts + tm - 1) // tm
    tile_end = jnp.cumsum(ntile_e)
    tile_start = tile_end - ntile_e
    n_active = tile_end[-1]
    n_tiles = (2 * t) // tm + n_experts
    n_slots = n_tiles * tm
    slot = tile_start[eidx] * tm + rank
    pair = jnp.arange(2 * t, dtype=jnp.int32)
    pair_of_slot = jnp.zeros((n_slots,), jnp.int32).at[slot].set(pair, unique_indices=True)
    tok_of_slot = pair_of_slot // 2
    dst_of_slot = (pair_of_slot % 2) * t + pair_of_slot // 2
    tiles = jnp.arange(n_tiles, dtype=jnp.int32)
    te = jnp.searchsorted(tile_end, jnp.minimum(tiles, n_active - 1), side="right").astype(jnp.int32)
    te = jnp.minimum(te, n_experts - 1)
    n_valid = jnp.clip(counts[te] - (tiles - tile_start[te]) * tm, 0, tm)
    n_valid = jnp.where(tiles < n_active, n_valid, 0).astype(jnp.int32)
    return (te, n_active.reshape(1).astype(jnp.int32), n_valid, tok_of_slot.reshape(n_tiles, 1, tm),
            dst_of_slot.reshape(n_tiles, 1, tm))


def _final_kernel(h_ref, y0_ref, y1_ref, info_ref, g_ref, o_ref):
    info = info_ref[...]
    x = h_ref[...] + info[:, 0:1] * y0_ref[...] + info[:, 1:2] * y1_ref[...]
    ms = jnp.mean(x * x, axis=-1, keepdims=True)
    o_ref[...] = x * lax.rsqrt(ms + EPS) * g_ref[...]


def final_combine(h, y, info, gain, *, tm=512):
    t, d = h.shape
    tm = min(tm, t)
    nt = t // tm
    return pl.pallas_call(
        _final_kernel,
        grid=(nt,),
        in_specs=[
            pl.BlockSpec((tm, d), lambda i: (i, 0)),
            pl.BlockSpec((tm, d), lambda i: (i, 0)),
            pl.BlockSpec((tm, d), lambda i: (nt + i, 0)),
            pl.BlockSpec((tm, LANES), lambda i: (i, 0)),
            pl.BlockSpec((1, d), lambda i: (0, 0)),
        ],
        out_specs=pl.BlockSpec((tm, d), lambda i: (i, 0)),
        out_shape=jax.ShapeDtypeStruct((t, d), F32),
        compiler_params=_cparams(("parallel",)),
        name="final_combine",
    )(h, y, y, info, gain.reshape(1, d).astype(F32))


def _softplus(x):
    return jnp.maximum(x, 0.0) + jnp.log1p(jnp.exp(-jnp.abs(x)))


def _gates_kernel(g_ref, p0_ref, p1_ref, col_ref, row_ref, *, mode, n_heads, rows, row_lo):
    x = g_ref[...]
    lane = lax.broadcasted_iota(jnp.int32, x.shape, 1)
    if mode == "gdn":
        logdec = -jnp.exp(p0_ref[...]) * _softplus(x + p1_ref[...])
        second = jax.nn.sigmoid(x)
    else:
        logdec = -_softplus(-(x + p0_ref[...]))
        second = x + p1_ref[...]
    logdec = jnp.where(lane < 2 * n_heads, logdec, 0.0)
    ri = lax.broadcasted_iota(jnp.int32, (CHUNK, CHUNK), 0)
    ci = lax.broadcasted_iota(jnp.int32, (CHUNK, CHUNK), 1)
    lower = (ri >= ci).astype(F32)
    upper = (ri <= ci).astype(F32)
    lane_c = lax.broadcasted_iota(jnp.int32, (CHUNK, LANES), 1)
    fwd_lane = lane_c < n_heads
    for c in range(rows // CHUNK):
        sl = slice(c * CHUNK, (c + 1) * CHUNK)
        ld = logdec[sl]
        cum = jnp.where(fwd_lane, _dot_hi(lower, ld), _dot_hi(upper, ld))
        sec = second[sl]
        if mode == "mlstm":
            sec = pltpu.roll(cum, 2 * n_heads, axis=1) - sec
        out = jnp.where(lane_c < 2 * n_heads, cum, jnp.where(lane_c < 4 * n_heads, sec, 0.0))
        col_ref[sl, :] = out
        row_ref[c] = jnp.transpose(out)[row_lo:row_lo + 2 * n_heads, :]


def gate_prep(g, p0, p1, *, mode, n_heads, rows=512):
    t = g.shape[0]
    rows = min(rows, t)
    assert t % rows == 0 and rows % CHUNK == 0
    row_lo = 0 if mode == "gdn" else 2 * n_heads
    return pl.pallas_call(
        functools.partial(_gates_kernel, mode=mode, n_heads=n_heads, rows=rows, row_lo=row_lo),
        grid=(t // rows,),
        in_specs=[
            pl.BlockSpec((rows, LANES), lambda i: (i, 0)),
            pl.BlockSpec((1, LANES), lambda i: (0, 0)),
            pl.BlockSpec((1, LANES), lambda i: (0, 0)),
        ],
        out_specs=[
            pl.BlockSpec((rows, LANES), lambda i: (i, 0)),
            pl.BlockSpec((rows // CHUNK, 2 * n_heads, CHUNK), lambda i: (i, 0, 0)),
        ],
        out_shape=[jax.ShapeDtypeStruct((t, LANES), F32), jax.ShapeDtypeStruct((t // CHUNK, 2 * n_heads, CHUNK), F32)],
        compiler_params=_cparams(("parallel",)),
        name="gate_prep_" + mode,
    )(g, p0, p1)


def _lane_vec(*parts):
    flat = jnp.concatenate([p.reshape(-1).astype(F32) for p in parts])
    return jnp.zeros((1, LANES), F32).at[0, :flat.shape[0]].set(flat)


def _col(tile, lane, j):
    return jnp.sum(jnp.where(lane == j, tile, 0.0), axis=1, keepdims=True)


def _tri_masks(direction):
    ri = lax.broadcasted_iota(jnp.int32, (CHUNK, CHUNK), 0)
    ci = lax.broadcasted_iota(jnp.int32, (CHUNK, CHUNK), 1)
    if direction == 0:
        return ri >= ci, ri > ci
    return ri <= ci, ri < ci


def _unit_tri_inverses(ms):
    ri = lax.broadcasted_iota(jnp.int32, (CHUNK, CHUNK), 0)
    ci = lax.broadcasted_iota(jnp.int32, (CHUNK, CHUNK), 1)
    eye = jnp.where(ri == ci, 1.0, 0.0)
    blk = ri // SUB == ci // SUB
    ds = [jnp.where(blk, m, 0.0) for m in ms]
    offs = [m - d for m, d in zip(ms, ds)]
    ps = [eye - d for d in ds]
    dps = ds
    for _ in range(3):
        dps = [_mm16(dp, dp) for dp in dps]
        ps = [p + _mm16(p, dp) for p, dp in zip(ps, dps)]
    ns = [_mm16(p, off) for p, off in zip(ps, offs)]
    n2s = [_mm16(n, n) for n in ns]
    qs = [(eye - n) + _mm16(eye - n, n2) for n, n2 in zip(ns, n2s)]
    return [_mm16(q, p) for q, p in zip(qs, ps)]


def _gdn_kernel(q_ref, k_ref, v_ref, z_ref, cwq_ref, cwk_ref, cwv_ref, gc_ref, gr_ref, gn_ref, o_ref,
                pad_ref, qn_ref, kn_ref, vn_ref, aq_ref, b_ref, egl_ref, oacc_ref,
                *, seq, n_heads, conv_rows, dk, group):
    h = pl.program_id(1)
    n_chunks = seq // CHUNK
    half = GDN_CONV // 2

    for src, cw_ref, dst, kind in ((q_ref, cwq_ref, qn_ref, "q"), (k_ref, cwk_ref, kn_ref, "k"), (v_ref, cwv_ref, vn_ref, "v")):
        pad_ref[0:8, :] = jnp.zeros((8, LANES), F32)
        pad_ref[seq + 8:seq + 16, :] = jnp.zeros((8, LANES), F32)
        pad_ref[8:seq + 8, :] = src[0].astype(F32)
        w = cw_ref[...]

        def conv_block(i, c, w=w, dst=dst, kind=kind):
            t0 = pl.multiple_of(i * conv_rows, conv_rows)
            win = pad_ref[pl.ds(t0, conv_rows + 16), :]
            acc = jnp.zeros((conv_rows, LANES), F32)
            for j in range(GDN_CONV):
                off = 8 - half + j
                acc = acc + win[off:off + conv_rows, :] * w[j:j + 1, :]
            y = _silu(acc)
            if kind != "v":
                y = y * lax.rsqrt(jnp.sum(y * y, axis=-1, keepdims=True) + EPS)
            if kind == "q":
                y = y * (dk ** -0.5)
            dst[pl.ds(t0, conv_rows), :] = y
            return c

        lax.fori_loop(0, seq // conv_rows, conv_block, 0)

    oacc_ref[...] = jnp.zeros_like(oacc_ref)
    lane = lax.broadcasted_iota(jnp.int32, (CHUNK, LANES), 1)

    def prepare(g, carry):
        chunks = []
        for j in range(group):
            c = g * group + j
            r0 = pl.multiple_of(c * CHUNK, CHUNK)
            qc = qn_ref[pl.ds(r0, CHUNK), :]
            kc = kn_ref[pl.ds(r0, CHUNK), :]
            chunks.append(dict(c=c, r0=r0, tile=gc_ref[0, pl.ds(r0, CHUNK), :], qc=qc, kc=kc,
                               vc=vn_ref[pl.ds(r0, CHUNK), :], q16=qc.astype(BF16), k16=kc.astype(BF16)))
        for ch in chunks:
            ch["kk"] = _dot_nt(ch["k16"], ch["k16"])
        for ch in chunks:
            ch["qk"] = _dot_nt(ch["q16"], ch["k16"])
        chains = []
        for ch in chunks:
            kt = jnp.transpose(ch["kc"])
            for direction in range(2):
                jd = direction * n_heads + h
                incl, strict = _tri_masks(direction)
                gcol = _col(ch["tile"], lane, jd)
                bcol = _col(ch["tile"], lane, 2 * n_heads + jd)
                grow = gr_ref[0, ch["c"], pl.ds(jd, 1), :]
                glast = gcol[CHUNK - 1:CHUNK] if direction == 0 else gcol[0:1]
                gam = jnp.exp(jnp.where(incl, gcol - grow, -jnp.inf))
                eg = jnp.exp(gcol)
                rhs = jnp.concatenate([ch["vc"] * bcol, ch["kc"] * (bcol * eg)], axis=1)
                rhs_hi = rhs.astype(BF16)
                chains.append(dict(
                    ch=ch, direction=direction, m=jnp.where(strict, ch["kk"] * bcol * gam, 0.0),
                    rhs_hi=rhs_hi, rhs_lo=(rhs - rhs_hi.astype(F32)).astype(BF16),
                    lhs=jnp.concatenate([(kt * jnp.exp(glast - grow)).astype(BF16), (ch["qk"] * gam).astype(BF16)], axis=0),
                    qd=ch["qc"] * eg, egl=jnp.exp(glast)))
        tinvs = [t.astype(BF16) for t in _unit_tri_inverses([cn["m"] for cn in chains])]
        sols = [_dot(t, cn["rhs_hi"]) + _dot(t, cn["rhs_lo"]) for t, cn in zip(tinvs, chains)]
        for cn, sol in zip(chains, sols):
            uw = jnp.concatenate([sol[:, :LANES], -sol[:, LANES:]], axis=1).astype(BF16)
            cn["r"] = _dot(cn["lhs"], uw)
        for cn in chains:
            d, c, r = cn["direction"], cn["ch"]["c"], cn["r"]
            aq_ref[d, c, 0:dk, :] = r[:dk, LANES:].astype(BF16)
            aq_ref[d, c, dk:dk + CHUNK, :] = (cn["qd"] + r[dk:, LANES:]).astype(BF16)
            b_ref[d, c] = r[:dk, :LANES].astype(BF16)
            egl_ref[d, c] = jnp.broadcast_to(cn["egl"], (8, LANES))
            oacc_ref[pl.ds(cn["ch"]["r0"], CHUNK), :] += r[dk:, :LANES]
        return carry

    lax.fori_loop(0, n_chunks // group, prepare, 0)

    def scan_step(c, direction, state):
        r0 = pl.multiple_of(c * CHUNK, CHUNK)
        r = _dot(aq_ref[direction, c], state.astype(BF16))
        oacc_ref[pl.ds(r0, CHUNK), :] += r[dk:]
        return state * egl_ref[direction, c][0:1, :] + r[:dk] + b_ref[direction, c].astype(F32)

    def scan(i, carry):
        sf, sb = carry
        return scan_step(i, 0, sf), scan_step(n_chunks - 1 - i, 1, sb)

    zero = jnp.zeros((dk, LANES), F32)
    lax.fori_loop(0, n_chunks, scan, (zero, zero))

    def epilogue(i, c):
        t0 = pl.multiple_of(i * conv_rows, conv_rows)
        o = oacc_ref[pl.ds(t0, conv_rows), :]
        ms = jnp.mean(o * o, axis=-1, keepdims=True)
        y = o * lax.rsqrt(ms + EPS) * gn_ref[...]
        z = z_ref[0, pl.ds(t0, conv_rows), :].astype(F32)
        o_ref[0, pl.ds(t0, conv_rows), :] = (y * _silu(z)).astype(o_ref.dtype)
        return c

    lax.fori_loop(0, seq // conv_rows, epilogue, 0)


def gdn_mixer(proj, conv_w, gcol, grow, gdn_norm, *, batch, seq, n_heads, col0, conv_rows=256, group=8):
    hb = n_heads
    conv_rows = min(conv_rows, seq)
    n_chunks = seq // CHUNK
    dk = LANES
    group = math.gcd(group, n_chunks)

    def head_spec(group):
        return pl.BlockSpec((1, seq, LANES), lambda b, h, group=group: (b, 0, col0 + group * hb + h))

    def conv_spec(group):
        return pl.BlockSpec((GDN_CONV, LANES), lambda b, h, group=group: (0, group * hb + h))

    return pl.pallas_call(
        functools.partial(_gdn_kernel, seq=seq, n_heads=n_heads, conv_rows=conv_rows, dk=dk, group=group),
        grid=(batch, n_heads),
        in_specs=[
            head_spec(0), head_spec(1), head_spec(2), head_spec(3),
            conv_spec(0), conv_spec(1), conv_spec(2),
            pl.BlockSpec((1, seq, LANES), lambda b, h: (b, 0, 0)),
            pl.BlockSpec((1, n_chunks, 2 * n_heads, CHUNK), lambda b, h: (b, 0, 0, 0)),
            pl.BlockSpec((1, LANES), lambda b, h: (0, 0)),
        ],
        out_specs=pl.BlockSpec((1, seq, LANES), lambda b, h: (b, 0, h)),
        out_shape=jax.ShapeDtypeStruct((batch, seq, n_heads * LANES), BF16),
        scratch_shapes=[
            pltpu.VMEM((seq + 16, LANES), F32),
            pltpu.VMEM((seq, LANES), F32),
            pltpu.VMEM((seq, LANES), F32),
            pltpu.VMEM((seq, LANES), F32),
            pltpu.VMEM((2, n_chunks, dk + CHUNK, LANES), BF16),
            pltpu.VMEM((2, n_chunks, dk, LANES), BF16),
            pltpu.VMEM((2, n_chunks, 8, LANES), F32),
            pltpu.VMEM((seq, LANES), F32),
        ],
        compiler_params=_cparams(("parallel", "arbitrary")),
        name="gdn_mixer",
    )(proj, proj, proj, proj, conv_w, conv_w, conv_w,
      gcol.reshape(batch, seq, LANES), grow.reshape(batch, n_chunks, 2 * n_heads, CHUNK),
      gdn_norm.reshape(1, LANES).astype(F32))


def _mlstm_kernel(q_ref, k_ref, v_ref, og_ref, gc_ref, gr_ref, nrm_ref, o_ref,
                  hf_ref, hb_ref, st_ref, *, seq, n_heads, rows, dqk, dv):
    p = pl.program_id(1)
    n_chunks = seq // CHUNK
    lane = lax.broadcasted_iota(jnp.int32, (CHUNK, LANES), 1)
    ones = jnp.ones((CHUNK, dv), BF16)
    scale = dqk ** -0.5

    st_ref[...] = jnp.zeros_like(st_ref)

    def body(i, ms):
        chains = []
        for direction in range(2):
            c = i if direction == 0 else n_chunks - 1 - i
            r0 = pl.multiple_of(c * CHUNK, CHUNK)
            incl, _ = _tri_masks(direction)
            tile = gc_ref[0, pl.ds(r0, CHUNK), :]
            for hh in range(2):
                idx = direction * 2 + hh
                m = ms[idx]
                jd = direction * n_heads + 2 * p + hh
                bc = _col(tile, lane, jd)
                ac = _col(tile, lane, 2 * n_heads + jd)
                arow = gr_ref[0, c, pl.ds(jd, 1), :]
                blast = bc[CHUNK - 1:CHUNK] if direction == 0 else bc[0:1]
                dmat = jnp.where(incl, bc - arow, NEG)
                inter = bc + m
                mj = jnp.maximum(inter, jnp.max(dmat, axis=1, keepdims=True))
                wl = blast - ac
                m_new = jnp.maximum(blast + m, jnp.max(wl, axis=0, keepdims=True))
                k = k_ref[0, pl.ds(r0, CHUNK), hh * dqk:(hh + 1) * dqk]
                chains.append(dict(
                    idx=idx, r0=r0, hh=hh, dst=hf_ref if direction == 0 else hb_ref, mj=mj, m_new=m_new,
                    q=q_ref[0, pl.ds(r0, CHUNK), hh * dqk:(hh + 1) * dqk], k=k,
                    vaug=jnp.concatenate([v_ref[0, pl.ds(r0, CHUNK), hh * dv:(hh + 1) * dv], ones], axis=1),
                    gate=jnp.exp(dmat - mj) * scale, iw=jnp.exp(inter - mj), decay=jnp.exp(blast + m - m_new),
                    kw=(k.astype(F32) * (jnp.exp(wl - m_new) * scale)).astype(BF16), state=st_ref[idx]))
        for cn in chains:
            cn["qk"] = _dot_nt(cn["q"], cn["k"])
        for cn in chains:
            cn["qc"] = _dot(cn["q"], cn["state"].astype(BF16))
        for cn in chains:
            cn["upd"] = _dot_tn(cn["kw"], cn["vaug"])
        for cn in chains:
            cn["sv"] = _dot((cn["qk"] * cn["gate"]).astype(BF16), cn["vaug"])
        for cn in chains:
            num = cn["iw"] * cn["qc"] + cn["sv"]
            den = num[:, dv:dv + 1]
            hout = num[:, :dv] / jnp.maximum(jnp.abs(den), jnp.exp(-cn["mj"]))
            cn["dst"][pl.ds(cn["r0"], CHUNK), cn["hh"] * dv:(cn["hh"] + 1) * dv] = hout
            st_ref[cn["idx"]] = cn["decay"] * cn["state"] + cn["upd"]
        return tuple(cn["m_new"] for cn in chains)

    zero = jnp.zeros((1, 1), F32)
    lax.fori_loop(0, n_chunks, body, (zero, zero, zero, zero))

    def epilogue(i, c):
        t0 = pl.multiple_of(i * rows, rows)
        g = nrm_ref[0]
        og = og_ref[0, pl.ds(t0, rows), :].astype(F32)
        for hh in range(2):
            sl = slice(hh * dv, (hh + 1) * dv)
            x = hf_ref[pl.ds(t0, rows), sl] + hb_ref[pl.ds(t0, rows), sl]
            ms = jnp.mean(x * x, axis=-1, keepdims=True)
            y = x * lax.rsqrt(ms + EPS) * g[:, sl]
            o_ref[0, pl.ds(t0, rows), sl] = (y * jax.nn.sigmoid(og[:, sl])).astype(o_ref.dtype)
        return c

    lax.fori_loop(0, seq // rows, epilogue, 0)


def mlstm_mixer(proj, gcol, grow, mlstm_norm, *, batch, seq, n_heads, q_blk, k_blk, v_blk, o_blk, rows=256):
    dqk, dv = MLSTM_DQK, MLSTM_DV
    n_chunks = seq // CHUNK
    rows = min(rows, seq)
    npair = n_heads // 2
    return pl.pallas_call(
        functools.partial(_mlstm_kernel, seq=seq, n_heads=n_heads, rows=rows, dqk=dqk, dv=dv),
        grid=(batch, npair),
        in_specs=[
            pl.BlockSpec((1, seq, 2 * dqk), lambda b, p: (b, 0, q_blk + p)),
            pl.BlockSpec((1, seq, 2 * dqk), lambda b, p: (b, 0, k_blk + p)),
            pl.BlockSpec((1, seq, 2 * dv), lambda b, p: (b, 0, v_blk + p)),
            pl.BlockSpec((1, seq, 2 * dv), lambda b, p: (b, 0, o_blk + p)),
            pl.BlockSpec((1, seq, LANES), lambda b, p: (b, 0, 0)),
            pl.BlockSpec((1, n_chunks, 2 * n_heads, CHUNK), lambda b, p: (b, 0, 0, 0)),
            pl.BlockSpec((1, 1, 2 * dv), lambda b, p: (p, 0, 0)),
        ],
        out_specs=pl.BlockSpec((1, seq, 2 * dv), lambda b, p: (b, 0, p)),
        out_shape=jax.ShapeDtypeStruct((batch, seq, n_heads * dv), BF16),
        scratch_shapes=[
            pltpu.VMEM((seq, 2 * dv), F32),
            pltpu.VMEM((seq, 2 * dv), F32),
            pltpu.VMEM((4, dqk, 2 * dv), F32),
        ],
        compiler_params=_cparams(("parallel", "arbitrary")),
        name="mlstm_mixer",
    )(proj, proj, proj, proj, gcol.reshape(batch, seq, LANES), grow.reshape(batch, n_chunks, 2 * n_heads, CHUNK),
      mlstm_norm.reshape(npair, 1, 2 * dv).astype(F32))


def _dsw_kernel(slope_ref, q_ref, k_ref, v_ref, o_ref, qf_ref, kf_ref, vf_ref, acc_ref, m_ref, l_ref,
                *, seq, pairs, qb, group, rows):
    slope = slope_ref[pl.program_id(1)]
    scale = LANES ** -0.5

    def to_f32(i, c):
        sl = pl.ds(pl.multiple_of(i * rows, rows), rows)
        qf_ref[sl, :] = q_ref[0, sl, :].astype(F32)
        kf_ref[sl, :] = k_ref[0, sl, :].astype(F32)
        vf_ref[sl, :] = v_ref[0, sl, :].astype(F32)
        return c

    lax.fori_loop(0, seq // rows, to_f32, 0)

    for branch, (window, dil) in enumerate(pairs):
        u_len = seq // dil
        half = window // (2 * dil)
        kb = qb + 2 * half
        n_blk = u_len // qb
        rel = lax.broadcasted_iota(jnp.int32, (qb, kb), 1) - lax.broadcasted_iota(jnp.int32, (qb, kb), 0)

        def blocks(g, c, dil=dil, half=half, kb=kb, n_blk=n_blk, u_len=u_len, rel=rel, branch=branch):
            work = []
            for j in range(group):
                bidx = g * group + j
                cls = bidx // n_blk
                u0 = (bidx % n_blk) * qb
                ks = jnp.clip(u0 - half, 0, u_len - kb)
                work.append(dict(
                    rows=pl.ds(cls + dil * u0, qb, stride=dil), dist=jnp.abs(rel + (ks - u0)),
                    q=qf_ref[pl.ds(cls + dil * u0, qb, stride=dil), :].astype(BF16),
                    k=kf_ref[pl.ds(cls + dil * ks, kb, stride=dil), :].astype(BF16),
                    v=vf_ref[pl.ds(cls + dil * ks, kb, stride=dil), :].astype(BF16)))
            for w in work:
                w["s"] = _dot_nt(w["q"], w["k"])
            for w in work:
                s = w["s"] * scale - (slope * float(dil)) * w["dist"].astype(F32)
                s = jnp.where(w["dist"] <= half, s, NEG)
                w["mx"] = jnp.max(s, axis=-1, keepdims=True)
                p = jnp.exp(s - w["mx"])
                w["den"] = jnp.sum(p, axis=-1, keepdims=True)
                w["p"] = p.astype(BF16)
            for w in work:
                w["o"] = _dot(w["p"], w["v"])
            for w in work:
                mx = jnp.broadcast_to(w["mx"], (qb, LANES))
                den = jnp.broadcast_to(w["den"], (qb, LANES))
                if branch == 0:
                    acc_ref[w["rows"], :] = w["o"]
                    m_ref[w["rows"], :] = mx
                    l_ref[w["rows"], :] = den
                else:
                    m_old = m_ref[w["rows"], :]
                    m_new = jnp.maximum(m_old, mx)
                    a = jnp.exp(m_old - m_new)
                    b = jnp.exp(mx - m_new)
                    acc_ref[w["rows"], :] = acc_ref[w["rows"], :] * a + w["o"] * b
                    l_ref[w["rows"], :] = l_ref[w["rows"], :] * a + den * b
                    m_ref[w["rows"], :] = m_new
            return c

        lax.fori_loop(0, dil * n_blk // group, blocks, 0)

    def finish(i, c):
        sl = pl.ds(pl.multiple_of(i * rows, rows), rows)
        o_ref[0, sl, :] = (acc_ref[sl, :] / l_ref[sl, :]).astype(o_ref.dtype)
        return c

    lax.fori_loop(0, seq // rows, finish, 0)


def dsw_attention(proj, slopes, *, batch, seq, n_heads, q_blk, k_blk, v_blk, pairs, qb=128, group=4, rows=512):
    rows = min(rows, seq)
    for window, dil in pairs:
        u_len = seq // dil
        assert seq % dil == 0 and u_len % qb == 0 and u_len >= qb + window // dil
        assert (dil * (u_len // qb)) % group == 0

    def spec(blk):
        return pl.BlockSpec((1, seq, LANES), lambda b, h, s, blk=blk: (b, 0, blk + h))

    return pl.pallas_call(
        functools.partial(_dsw_kernel, seq=seq, pairs=pairs, qb=qb, group=group, rows=rows),
        grid_spec=pltpu.PrefetchScalarGridSpec(
            num_scalar_prefetch=1,
            grid=(batch, n_heads),
            in_specs=[spec(q_blk), spec(k_blk), spec(v_blk)],
            out_specs=pl.BlockSpec((1, seq, LANES), lambda b, h, s: (b, 0, h)),
            scratch_shapes=[pltpu.VMEM((seq, LANES), F32) for _ in range(6)],
        ),
        out_shape=jax.ShapeDtypeStruct((batch, seq, n_heads * LANES), BF16),
        compiler_params=_cparams(("parallel", "arbitrary")),
        name="dsw_attention",
    )(slopes, proj, proj, proj)


def _mla_kernel(q_ref, tq_ref, kn_ref, v_ref, kr_ref, krr_ref, ck_ref, sk_ref, o_ref, k_scr, v_scr, *, seq, kvb):
    @pl.when(pl.program_id(2) == 0)
    def _():
        k_scr[:, :LANES] = kn_ref[0]
        rk = kr_ref[0].astype(F32) * ck_ref[...] + krr_ref[0].astype(F32) * sk_ref[...]
        k_scr[:, LANES:] = rk.astype(BF16)
        v_scr[:, :LANES] = v_ref[0]
        v_scr[:, LANES:] = jnp.ones((seq, LANES), BF16)

    q = (q_ref[0].astype(F32) * tq_ref[...]).astype(BF16)
    mx = jnp.full((q.shape[0], 1), -jnp.inf, F32)
    acc = jnp.zeros((q.shape[0], 2 * LANES), F32)
    for b in range(seq // kvb):
        s = _dot_nt(q, k_scr[b * kvb:(b + 1) * kvb, :])
        mx_new = jnp.maximum(mx, jnp.max(s, axis=-1, keepdims=True))
        p = jnp.exp2(s - mx_new).astype(BF16)
        acc = acc * jnp.exp2(mx - mx_new) + _dot(p, v_scr[b * kvb:(b + 1) * kvb, :])
        mx = mx_new
    o_ref[0] = (acc[:, :LANES] / acc[:, LANES:LANES + 1]).astype(o_ref.dtype)


def mla_attention(q_up, kv_up, proj, tq_tab, ck_tab, sk_tab, *, batch, seq, n_heads, kr_blk, krr_blk, tq=512, kvb=512):
    tq = min(tq, seq)
    kvb = min(kvb, seq)
    return pl.pallas_call(
        functools.partial(_mla_kernel, seq=seq, kvb=kvb),
        grid=(batch, n_heads, seq // tq),
        in_specs=[
            pl.BlockSpec((1, tq, 2 * LANES), lambda b, h, i: (b, i, h)),
            pl.BlockSpec((tq, 2 * LANES), lambda b, h, i: (i, 0)),
            pl.BlockSpec((1, seq, LANES), lambda b, h, i: (b, 0, 2 * h)),
            pl.BlockSpec((1, seq, LANES), lambda b, h, i: (b, 0, 2 * h + 1)),
            pl.BlockSpec((1, seq, LANES), lambda b, h, i: (b, 0, kr_blk)),
            pl.BlockSpec((1, seq, LANES), lambda b, h, i: (b, 0, krr_blk)),
            pl.BlockSpec((seq, LANES), lambda b, h, i: (0, 0)),
            pl.BlockSpec((seq, LANES), lambda b, h, i: (0, 0)),
        ],
        out_specs=pl.BlockSpec((1, tq, LANES), lambda b, h, i: (b, i, h)),
        out_shape=jax.ShapeDtypeStruct((batch, seq, n_heads * LANES), BF16),
        scratch_shapes=[pltpu.VMEM((seq, 2 * LANES), BF16), pltpu.VMEM((seq, 2 * LANES), BF16)],
        compiler_params=_cparams(("parallel", "parallel", "arbitrary")),
        name="mla_attention",
    )(q_up, tq_tab, kv_up, kv_up, proj, proj, ck_tab, sk_tab)


def _rot_cols(w):
    half = w.shape[-1] // 2
    return jnp.concatenate([-w[..., half:], w[..., :half]], axis=-1)


def layer_even(h, batch, seq, mix_norm, w_in, conv_w, a_log, dt_bias, gdn_norm, w_out, ffn_norm, w_gate, w_up, w_down,
               *, tm=512, tm_in=1024, tf=512):
    d = h.shape[1]
    nh = GDN_HEADS
    qk = nh * LANES
    gate0 = 4 * qk
    att0 = gate0 + 4 * nh
    w_gates = jnp.zeros((d, LANES), F32).at[:, :4 * nh].set(w_in[:, gate0:att0]).astype(BF16)
    proj_a = norm_matmul(h, mix_norm, w_in[:, :gate0].astype(BF16), out_dtype=BF16, tm=tm_in, name="l0_in_proj_gdn")
    proj_b = norm_matmul(h, mix_norm, w_in[:, att0:].astype(BF16), out_dtype=BF16, tm=tm_in, name="l0_in_proj_dsw")
    graw = norm_matmul(h, mix_norm, w_gates, out_dtype=F32, tm=tm_in, name="l0_gate_proj")
    gcol, grow = gate_prep(graw, _lane_vec(a_log), _lane_vec(dt_bias), mode="gdn", n_heads=nh)
    o_a = gdn_mixer(proj_a.reshape(batch, seq, gate0), conv_w.astype(F32), gcol, grow, gdn_norm,
                    batch=batch, seq=seq, n_heads=nh, col0=0)
    slopes = (2.0 ** (-8.0 * jnp.arange(1, DSW_HEADS + 1, dtype=F32) / DSW_HEADS)).astype(F32)
    o_b = dsw_attention(proj_b.reshape(batch, seq, proj_b.shape[1]), slopes, batch=batch, seq=seq, n_heads=DSW_HEADS,
                        q_blk=0, k_blk=DSW_HEADS, v_blk=2 * DSW_HEADS, pairs=DSW_PAIRS)
    na = nh * LANES
    h = proj_residual(o_a.reshape(batch * seq, na), o_b.reshape(batch * seq, -1), w_out[:na].astype(BF16),
                      w_out[na:].astype(BF16), h,
                      tm=tm, name="l0_out_proj")
    return ffn_residual(h, ffn_norm, _tile_cols(w_gate, tf), _tile_cols(w_up, tf), w_down.astype(BF16), tm=tm)


def layer_odd_mixers(h, batch, seq, mix_norm, w_in, q_norm, kv_norm, w_uq, w_ukv, ig_bias, fg_bias, mlstm_norm, w_out,
                     *, tm=512, tm_in=1024):
    d = h.shape[1]
    nh = MLSTM_HEADS
    sizes = (MLA_Q_RANK, MLA_KV_RANK, MLA_ROPE, nh * MLSTM_DQK, nh * MLSTM_DQK, nh * MLSTM_DV, nh * MLSTM_DV, 2 * nh, 2 * nh)
    cuts = np.cumsum((0,) + sizes)
    w_cq, w_ckv, w_kr, w_mq, w_mk, w_mv, w_mo, w_mi, w_mf = (w_in[:, cuts[i]:cuts[i + 1]] for i in range(9))
    w_krot = _rot_cols(w_kr)
    w_main = jnp.concatenate([w_cq, w_kr, w_kr, w_krot, w_krot, w_ckv, w_mq, w_mk, w_mv, w_mo], axis=1).astype(BF16)
    w_gates = jnp.zeros((d, LANES), F32).at[:, :2 * nh].set(w_mf).at[:, 2 * nh:4 * nh].set(w_mi).astype(BF16)
    proj = norm_matmul(h, mix_norm, w_main, out_dtype=BF16, tm=tm_in, name="l1_in_proj")
    graw = norm_matmul(h, mix_norm, w_gates, out_dtype=F32, tm=tm_in, name="l1_gate_proj")
    gcol, grow = gate_prep(graw, _lane_vec(fg_bias), _lane_vec(jnp.zeros((2 * nh,), F32), ig_bias), mode="mlstm", n_heads=nh)
    width = proj.shape[1]
    proj3 = proj.reshape(batch, seq, width)

    hq = MLA_HEADS
    wq = w_uq.reshape(MLA_Q_RANK, hq, MLA_NOPE + MLA_ROPE)
    wq_rope = wq[:, :, MLA_NOPE:]
    wq_all = jnp.concatenate([wq[:, :, :MLA_NOPE], wq_rope, _rot_cols(wq_rope)], axis=-1).reshape(MLA_Q_RANK, hq * 2 * LANES)
    q_up = norm_matmul(proj, q_norm, wq_all.astype(BF16), out_dtype=BF16, k_block=0, k_width=MLA_Q_RANK, tm=tm_in, name="mla_q_up")
    kv_blk = (MLA_Q_RANK + 2 * LANES) // MLA_KV_RANK
    kv_up = norm_matmul(proj, kv_norm, w_ukv.astype(BF16), out_dtype=BF16, k_block=kv_blk, k_width=MLA_KV_RANK, tm=tm_in, name="mla_kv_up")
    pos = jnp.arange(seq, dtype=F32)
    freqs = ROPE_THETA ** (-jnp.arange(0, MLA_ROPE, 2, dtype=F32) / MLA_ROPE)
    ang = pos[:, None] * freqs[None, :]
    cos, sin = jnp.cos(ang), jnp.sin(ang)
    scale = LOG2E * (MLA_NOPE + MLA_ROPE) ** -0.5
    tq_tab = scale * jnp.concatenate([jnp.ones((seq, MLA_NOPE), F32), cos, cos, sin, sin], axis=1)
    ck_tab = jnp.concatenate([cos] * 4, axis=1)
    sk_tab = jnp.concatenate([sin] * 4, axis=1)
    kr_blk = MLA_Q_RANK // LANES
    o_c = mla_attention(q_up.reshape(batch, seq, -1), kv_up.reshape(batch, seq, -1), proj3, tq_tab, ck_tab, sk_tab,
                        batch=batch, seq=seq, n_heads=hq, kr_blk=kr_blk, krr_blk=kr_blk + 1)

    mq0 = kr_blk + 2 + MLA_KV_RANK // LANES
    mk0 = mq0 + nh * MLSTM_DQK // LANES
    mv0 = mk0 + nh * MLSTM_DQK // LANES
    mo0 = mv0 + nh * MLSTM_DV // LANES
    o_d = mlstm_mixer(proj3, gcol, grow, mlstm_norm, batch=batch, seq=seq, n_heads=nh,
                      q_blk=mq0, k_blk=mk0, v_blk=mv0 // 2, o_blk=mo0 // 2)
    nc = hq * MLA_V
    return proj_residual(o_c.reshape(batch * seq, nc), o_d.reshape(batch * seq, -1), w_out[:nc].astype(BF16),
                         w_out[nc:].astype(BF16), h, tm=tm, name="l1_out_proj")


def _tile_cols(w, tf):
    *lead, k, n = w.shape
    tf = min(tf, n)
    return jnp.moveaxis(w.astype(BF16).reshape(*lead, k, n // tf, tf), -2, -3)


def moe_block(h, ffn_norm, w_router, b_router, we_gate, we_up, we_down, final_norm, *, tm=896, tf=512, tm_tok=512):
    t = h.shape[0]
    n_experts = we_gate.shape[0]
    tm = min(tm, t)
    xn, info = router(h, ffn_norm, w_router, b_router, tm=tm_tok)
    te, n_active, n_valid, tok_of_slot, dst_of_slot = moe_dispatch_tables(info, tm=tm, n_experts=n_experts)
    y = moe_experts(xn, te, n_active, n_valid, tok_of_slot, dst_of_slot, _tile_cols(we_gate, tf), _tile_cols(we_up, tf),
                    we_down.astype(BF16), tm=tm)
    return final_combine(h, y, info, final_norm, tm=tm_tok)


def kernel(x, even_mix_norm, even_w_in, even_conv_w, even_a_log, even_dt_bias, even_gdn_norm, even_w_out,
           even_ffn_norm, even_w_gate, even_w_up, even_w_down, odd_mix_norm, odd_w_in, odd_q_norm, odd_kv_norm,
           odd_w_uq, odd_w_ukv, odd_ig_bias, odd_fg_bias, odd_mlstm_norm, odd_w_out, odd_ffn_norm,
           odd_w_router, odd_b_router, odd_we_gate, odd_we_up, odd_we_down, final_norm):
    batch, seq, d = x.shape
    h = x.reshape(batch * seq, d)
    h = layer_even(h, batch, seq, even_mix_norm[0], even_w_in[0], even_conv_w[0], even_a_log[0], even_dt_bias[0],
                   even_gdn_norm[0], even_w_out[0], even_ffn_norm[0], even_w_gate[0], even_w_up[0], even_w_down[0])
    h = layer_odd_mixers(h, batch, seq, odd_mix_norm[0], odd_w_in[0], odd_q_norm[0], odd_kv_norm[0], odd_w_uq[0],
                         odd_w_ukv[0], odd_ig_bias[0], odd_fg_bias[0], odd_mlstm_norm[0], odd_w_out[0])
    out = moe_block(h, odd_ffn_norm[0], odd_w_router[0], odd_b_router[0], odd_we_gate[0], odd_we_up[0],
                    odd_we_down[0], final_norm)
    return out.reshape(batch, seq, d)
```

```python
import functools
import math

import jax
import jax.numpy as jnp
import numpy as np
from jax import lax
from jax.experimental import pallas as pl
from jax.experimental.pallas import tpu as pltpu

F32 = jnp.float32
BF16 = jnp.bfloat16
EPS = 1e-6
NEG = -1e30
HI = lax.Precision.HIGHEST
LOG2E = math.log2(math.e)

LANES = 128
CHUNK = 64
SUB = 16
VMEM_LIMIT = 56 * 1024 * 1024

GDN_HEADS = 8
GDN_CONV = 5
DSW_HEADS = 8
DSW_PAIRS = ((128, 1), (512, 4), (2048, 16))
MLA_HEADS = 8
MLA_Q_RANK = 768
MLA_KV_RANK = 512
MLA_NOPE = 128
MLA_ROPE = 64
MLA_V = 128
ROPE_THETA = 10000.0
MLSTM_HEADS = 8
MLSTM_DQK = 64
MLSTM_DV = 128


def _cparams(sem):
    return pltpu.CompilerParams(dimension_semantics=sem, vmem_limit_bytes=VMEM_LIMIT)


def _dot(a, b):
    return jnp.dot(a, b, preferred_element_type=F32)


def _dot_nt(a, b):
    return lax.dot_general(a, b, (((1,), (1,)), ((), ())), preferred_element_type=F32)


def _dot_tn(a, b):
    return lax.dot_general(a, b, (((0,), (0,)), ((), ())), preferred_element_type=F32)


def _dot_hi(a, b):
    return jnp.dot(a, b, preferred_element_type=F32, precision=HI)


def _mm16(a, b):
    return _dot(a.astype(BF16), b.astype(BF16))


def _nm_kernel(x_ref, g_ref, w_ref, o_ref, xn_ref, *, use_norm):
    @pl.when(pl.program_id(1) == 0)
    def _():
        x = x_ref[...].astype(F32)
        if use_norm:
            ms = jnp.mean(x * x, axis=-1, keepdims=True)
            x = x * lax.rsqrt(ms + EPS) * g_ref[...]
        xn_ref[...] = x.astype(BF16)

    o_ref[...] = _dot(xn_ref[...], w_ref[...]).astype(o_ref.dtype)


def norm_matmul(x, gain, w, *, out_dtype, k_block=0, k_width=None, tm=512, tn=512, use_norm=True, name="norm_matmul"):
    t = x.shape[0]
    kw = x.shape[1] if k_width is None else k_width
    n = w.shape[1]
    tm = min(tm, t)
    tn = min(tn, n)
    assert t % tm == 0 and n % tn == 0 and w.shape[0] == kw
    g2 = gain.reshape(1, kw).astype(F32)
    return pl.pallas_call(
        functools.partial(_nm_kernel, use_norm=use_norm),
        grid=(t // tm, n // tn),
        in_specs=[
            pl.BlockSpec((tm, kw), lambda i, j: (i, k_block)),
            pl.BlockSpec((1, kw), lambda i, j: (0, 0)),
            pl.BlockSpec((kw, tn), lambda i, j: (0, j)),
        ],
        out_specs=pl.BlockSpec((tm, tn), lambda i, j: (i, j)),
        out_shape=jax.ShapeDtypeStruct((t, n), out_dtype),
        scratch_shapes=[pltpu.VMEM((tm, kw), BF16)],
        compiler_params=_cparams(("parallel", "arbitrary")),
        name=name,
    )(x, g2, w)


def _proj_res_kernel(a_ref, b_ref, w1_ref, w2_ref, r_ref, o_ref):
    acc = _dot(a_ref[...], w1_ref[...]) + _dot(b_ref[...], w2_ref[...])
    o_ref[...] = r_ref[...] + acc


def proj_residual(a, b, w1, w2, res, *, tm=512, name="proj_residual"):
    t, ka = a.shape
    kb = b.shape[1]
    n = w1.shape[1]
    tm = min(tm, t)
    assert t % tm == 0
    return pl.pallas_call(
        _proj_res_kernel,
        grid=(t // tm,),
        in_specs=[
            pl.BlockSpec((tm, ka), lambda i: (i, 0)),
            pl.BlockSpec((tm, kb), lambda i: (i, 0)),
            pl.BlockSpec((ka, n), lambda i: (0, 0)),
            pl.BlockSpec((kb, n), lambda i: (0, 0)),
            pl.BlockSpec((tm, n), lambda i: (i, 0)),
        ],
        out_specs=pl.BlockSpec((tm, n), lambda i: (i, 0)),
        out_shape=jax.ShapeDtypeStruct((t, n), F32),
        compiler_params=_cparams(("parallel",)),
        name=name,
    )(a, b, w1, w2, res)


def _silu(a):
    return a * jax.nn.sigmoid(a)


def _ffn_kernel(h_ref, g_ref, wg_ref, wu_ref, wd_ref, o_ref, xn_ref, acc_ref):
    j = pl.program_id(1)

    @pl.when(j == 0)
    def _():
        x = h_ref[...]
        ms = jnp.mean(x * x, axis=-1, keepdims=True)
        xn_ref[...] = (x * lax.rsqrt(ms + EPS) * g_ref[...]).astype(BF16)
        acc_ref[...] = jnp.zeros_like(acc_ref)

    xn = xn_ref[...]
    mid = (_silu(_dot(xn, wg_ref[0])) * _dot(xn, wu_ref[0])).astype(BF16)
    acc_ref[...] += _dot(mid, wd_ref[...])

    @pl.when(j == pl.num_programs(1) - 1)
    def _():
        o_ref[...] = h_ref[...] + acc_ref[...]


def ffn_residual(h, gain, wg, wu, wd, *, tm=512, name="ffn_swiglu"):
    t, d = h.shape
    nj, _, tf = wg.shape
    tm = min(tm, t)
    assert t % tm == 0
    return pl.pallas_call(
        _ffn_kernel,
        grid=(t // tm, nj),
        in_specs=[
            pl.BlockSpec((tm, d), lambda i, j: (i, 0)),
            pl.BlockSpec((1, d), lambda i, j: (0, 0)),
            pl.BlockSpec((1, d, tf), lambda i, j: (j, 0, 0)),
            pl.BlockSpec((1, d, tf), lambda i, j: (j, 0, 0)),
            pl.BlockSpec((tf, d), lambda i, j: (j, 0)),
        ],
        out_specs=pl.BlockSpec((tm, d), lambda i, j: (i, 0)),
        out_shape=jax.ShapeDtypeStruct((t, d), F32),
        scratch_shapes=[pltpu.VMEM((tm, d), BF16), pltpu.VMEM((tm, d), F32)],
        compiler_params=_cparams(("parallel", "arbitrary")),
        name=name,
    )(h, gain.reshape(1, d).astype(F32), wg, wu, wd)


def _router_kernel(h_ref, g_ref, wr_ref, br_ref, xn_ref, info_ref, *, n_experts):
    x = h_ref[...]
    ms = jnp.mean(x * x, axis=-1, keepdims=True)
    xn = x * lax.rsqrt(ms + EPS) * g_ref[...]
    bits = pltpu.bitcast(xn.astype(BF16).astype(F32), jnp.uint32)
    half = bits.shape[1] // 2
    xn_ref[...] = (bits[:, :half] >> 16) | (bits[:, half:] & jnp.uint32(0xFFFF0000))
    logits = _dot_hi(xn, wr_ref[...]) + br_ref[...]
    lane = lax.broadcasted_iota(jnp.int32, logits.shape, 1)
    real = lane < n_experts
    lg = jnp.where(real, logits, -jnp.inf)
    mx = jnp.max(lg, axis=-1, keepdims=True)
    ex = jnp.exp(lg - mx)
    probs = ex / jnp.sum(ex, axis=-1, keepdims=True)
    probs = jnp.where(real, probs, -1.0)
    v1 = jnp.max(probs, axis=-1, keepdims=True)
    i1 = jnp.min(jnp.where(probs == v1, lane, LANES), axis=-1, keepdims=True)
    rest = jnp.where(lane == i1, -1.0, probs)
    v2 = jnp.max(rest, axis=-1, keepdims=True)
    i2 = jnp.min(jnp.where(rest == v2, lane, LANES), axis=-1, keepdims=True)
    tot = v1 + v2
    info = jnp.where(lane == 0, v1 / tot, 0.0)
    info = jnp.where(lane == 1, v2 / tot, info)
    info = jnp.where(lane == 2, i1.astype(F32), info)
    info = jnp.where(lane == 3, i2.astype(F32), info)
    info_ref[...] = info


def router(h, gain, w_router, b_router, *, tm=512):
    t, d = h.shape
    e = w_router.shape[1]
    tm = min(tm, t)
    wr = jnp.zeros((d, LANES), F32).at[:, :e].set(w_router.astype(F32))
    br = jnp.zeros((1, LANES), F32).at[0, :e].set(b_router.astype(F32))
    return pl.pallas_call(
        functools.partial(_router_kernel, n_experts=e),
        grid=(t // tm,),
        in_specs=[
            pl.BlockSpec((tm, d), lambda i: (i, 0)),
            pl.BlockSpec((1, d), lambda i: (0, 0)),
            pl.BlockSpec((d, LANES), lambda i: (0, 0)),
            pl.BlockSpec((1, LANES), lambda i: (0, 0)),
        ],
        out_specs=[pl.BlockSpec((tm, d // 2), lambda i: (i, 0)), pl.BlockSpec((tm, LANES), lambda i: (i, 0))],
        out_shape=[jax.ShapeDtypeStruct((t, d // 2), jnp.uint32), jax.ShapeDtypeStruct((t, LANES), F32)],
        compiler_params=_cparams(("parallel",)),
        name="router",
    )(h, gain.reshape(1, d).astype(F32), wr, br)


def _moe_kernel(te_ref, na_ref, nv_ref, tok_ref, tokn_ref, dst_ref, dstp_ref, xn_hbm, wg_ref, wu_ref, wd_ref, out_hbm,
                xrow_ref, xb_ref, acc_ref, sem_in, sem_out, *, tm, unroll, rows_per_step):
    i = pl.program_id(0)
    j = pl.program_id(1)
    nj = pl.num_programs(1)
    n_active = na_ref[0]
    active = i < n_active
    slot = i % 2
    n_valid = nv_ref[i]
    n_valid_prev = nv_ref[jnp.maximum(i - 1, 0)]

    def gather_copy(ids_ref, r, s):
        return pltpu.make_async_copy(xn_hbm.at[pl.ds(ids_ref[0, 0, r], 1)], xrow_ref.at[s, pl.ds(r, 1)], sem_in.at[s])

    def scatter_copy(ids_ref, r, s):
        return pltpu.make_async_copy(acc_ref.at[s, pl.ds(r, 1)], out_hbm.at[pl.ds(ids_ref[0, 0, r], 1)], sem_out.at[s])

    def for_rows(n, fn, unroll=1):
        def body(r, c):
            fn(r)
            return c

        if isinstance(n, int):
            lax.fori_loop(0, n, body, 0, unroll=unroll)
            return

        def body_blk(b, c):
            for u in range(unroll):
                fn(b * unroll + u)
            return c

        n_blk = n // unroll
        lax.fori_loop(0, n_blk, body_blk, 0)
        lax.fori_loop(n_blk * unroll, n, body, 0)

    @pl.when(active & (j == 0))
    def _():
        @pl.when(i == 0)
        def _():
            for_rows(tm, lambda r: gather_copy(tok_ref, r, 0).start(), unroll)

        for_rows(tm, lambda r: gather_copy(tok_ref, r, slot).wait(), unroll)
        words = xrow_ref[slot]
        half = words.shape[1]
        xb_ref[:, :half] = pltpu.bitcast(words << 16, F32).astype(BF16)
        xb_ref[:, half:] = pltpu.bitcast(words & jnp.uint32(0xFFFF0000), F32).astype(BF16)
        acc_ref[slot] = jnp.zeros((tm, acc_ref.shape[2]), F32)

    @pl.when(active)
    def _():
        for u in range(rows_per_step):
            gather_copy(tokn_ref, j * rows_per_step + u, 1 - slot).start()
        xb = xb_ref[...]
        mid = (_silu(_dot(xb, wg_ref[0, 0])) * _dot(xb, wu_ref[0, 0])).astype(BF16)
        acc_ref[slot] += _dot(mid, wd_ref[0])

    @pl.when(active & (j == nj - 1))
    def _():
        @pl.when(i > 0)
        def _():
            for_rows(n_valid_prev, lambda r: scatter_copy(dstp_ref, r, 1 - slot).wait(), unroll)

        for_rows(n_valid, lambda r: scatter_copy(dst_ref, r, slot).start(), unroll)

        @pl.when(i == n_active - 1)
        def _():
            for_rows(n_valid, lambda r: scatter_copy(dst_ref, r, slot).wait(), unroll)
            for_rows(tm, lambda r: gather_copy(tokn_ref, r, 1 - slot).wait(), unroll)


def moe_experts(xn, tile_expert, n_active, n_valid, tok_of_slot, dst_of_slot, wg, wu, wd, *, tm):
    t = xn.shape[0]
    d = 2 * xn.shape[1]
    e, nj, _, tf = wg.shape
    n_tiles = tok_of_slot.shape[0]
    assert tm % nj == 0, "the row prefetch is spread evenly over the grid steps of a tile"

    def w_in_map(i, j, te, na, nv):
        return (te[i], jnp.where(i < na[0], j, nj - 1), 0, 0)

    def w_dn_map(i, j, te, na, nv):
        return (te[i], jnp.where(i < na[0], j, nj - 1), 0)

    def ids_spec(shift):
        return pl.BlockSpec((1, 1, tm), lambda i, j, te, na, nv: (jnp.clip(i + shift, 0, n_tiles - 1), 0, 0),
                            memory_space=pltpu.SMEM)

    grid_spec = pltpu.PrefetchScalarGridSpec(
        num_scalar_prefetch=3,
        grid=(n_tiles, nj),
        in_specs=[
            ids_spec(0), ids_spec(1), ids_spec(0), ids_spec(-1),
            pl.BlockSpec(memory_space=pl.ANY),
            pl.BlockSpec((1, 1, d, tf), w_in_map),
            pl.BlockSpec((1, 1, d, tf), w_in_map),
            pl.BlockSpec((1, tf, d), w_dn_map),
        ],
        out_specs=pl.BlockSpec(memory_space=pl.ANY),
        scratch_shapes=[
            pltpu.VMEM((2, tm, d // 2), jnp.uint32),
            pltpu.VMEM((tm, d), BF16),
            pltpu.VMEM((2, tm, d), F32),
            pltpu.SemaphoreType.DMA((2,)),
            pltpu.SemaphoreType.DMA((2,)),
        ],
    )
    return pl.pallas_call(
        functools.partial(_moe_kernel, tm=tm, unroll=8 if tm % 8 == 0 else 1, rows_per_step=tm // nj),
        grid_spec=grid_spec,
        out_shape=jax.ShapeDtypeStruct((2 * t, d), F32),
        compiler_params=_cparams(("arbitrary", "arbitrary")),
        name="moe_experts",
    )(tile_expert, n_active, n_valid, tok_of_slot, tok_of_slot, dst_of_slot, dst_of_slot, xn, wg, wu, wd)


def moe_dispatch_tables(info, *, tm, n_experts):
    t = info.shape[0]
    eidx = info[:, 2:4].astype(jnp.int32).reshape(-1)
    onehot = (eidx[:, None] == jnp.arange(n_experts, dtype=jnp.int32)[None, :]).astype(jnp.int32)
    csum = jnp.cumsum(onehot, axis=0)
    rank = jnp.sum((csum - onehot) * onehot, axis=1)
    counts = csum[-1]
    ntile_e = (counts + tm - 1) // tm
    tile_end = jnp.cumsum(ntile_e)
    tile_start = tile_end - ntile_e
    n_active = tile_end[-1]
    n_tiles = (2 * t) // tm + n_experts
    n_slots = n_tiles * tm
    slot = tile_start[eidx] * tm + rank
    pair = jnp.arange(2 * t, dtype=jnp.int32)
    pair_of_slot = jnp.zeros((n_slots,), jnp.int32).at[slot].set(pair, unique_indices=True)
    tok_of_slot = pair_of_slot // 2
    dst_of_slot = (pair_of_slot % 2) * t + pair_of_slot // 2
    tiles = jnp.arange(n_tiles, dtype=jnp.int32)
    te = jnp.searchsorted(tile_end, jnp.minimum(tiles, n_active - 1), side="right").astype(jnp.int32)
    te = jnp.minimum(te, n_experts - 1)
    n_valid = jnp.clip(counts[te] - (tiles - tile_start[te]) * tm, 0, tm)
    n_valid = jnp.where(tiles < n_active, n_valid, 0).astype(jnp.int32)
    return (te, n_active.reshape(1).astype(jnp.int32), n_valid, tok_of_slot.reshape(n_tiles, 1, tm),
            dst_of_slot.reshape(n_tiles, 1, tm))


def _final_kernel(h_ref, y0_ref, y1_ref, info_ref, g_ref, o_ref):
    info = info_ref[...]
    x = h_ref[...] + info[:, 0:1] * y0_ref[...] + info[:, 1:2] * y1_ref[...]
    ms = jnp.mean(x * x, axis=-1, keepdims=True)
    o_ref[...] = x * lax.rsqrt(ms + EPS) * g_ref[...]


def final_combine(h, y, info, gain, *, tm=512):
    t, d = h.shape
    tm = min(tm, t)
    nt = t // tm
    return pl.pallas_call(
        _final_kernel,
        grid=(nt,),
        in_specs=[
            pl.BlockSpec((tm, d), lambda i: (i, 0)),
            pl.BlockSpec((tm, d), lambda i: (i, 0)),
            pl.BlockSpec((tm, d), lambda i: (nt + i, 0)),
            pl.BlockSpec((tm, LANES), lambda i: (i, 0)),
            pl.BlockSpec((1, d), lambda i: (0, 0)),
        ],
        out_specs=pl.BlockSpec((tm, d), lambda i: (i, 0)),
        out_shape=jax.ShapeDtypeStruct((t, d), F32),
        compiler_params=_cparams(("parallel",)),
        name="final_combine",
    )(h, y, y, info, gain.reshape(1, d).astype(F32))


def _softplus(x):
    return jnp.maximum(x, 0.0) + jnp.log1p(jnp.exp(-jnp.abs(x)))


def _gates_kernel(g_ref, p0_ref, p1_ref, col_ref, row_ref, *, mode, n_heads, rows, row_lo):
    x = g_ref[...]
    lane = lax.broadcasted_iota(jnp.int32, x.shape, 1)
    if mode == "gdn":
        logdec = -jnp.exp(p0_ref[...]) * _softplus(x + p1_ref[...])
        second = jax.nn.sigmoid(x)
    else:
        logdec = -_softplus(-(x + p0_ref[...]))
        second = x + p1_ref[...]
    logdec = jnp.where(lane < 2 * n_heads, logdec, 0.0)
    ri = lax.broadcasted_iota(jnp.int32, (CHUNK, CHUNK), 0)
    ci = lax.broadcasted_iota(jnp.int32, (CHUNK, CHUNK), 1)
    lower = (ri >= ci).astype(F32)
    upper = (ri <= ci).astype(F32)
    lane_c = lax.broadcasted_iota(jnp.int32, (CHUNK, LANES), 1)
    fwd_lane = lane_c < n_heads
    for c in range(rows // CHUNK):
        sl = slice(c * CHUNK, (c + 1) * CHUNK)
        ld = logdec[sl]
        cum = jnp.where(fwd_lane, _dot_hi(lower, ld), _dot_hi(upper, ld))
        sec = second[sl]
        if mode == "mlstm":
            sec = pltpu.roll(cum, 2 * n_heads, axis=1) - sec
        out = jnp.where(lane_c < 2 * n_heads, cum, jnp.where(lane_c < 4 * n_heads, sec, 0.0))
        col_ref[sl, :] = out
        row_ref[c] = jnp.transpose(out)[row_lo:row_lo + 2 * n_heads, :]


def gate_prep(g, p0, p1, *, mode, n_heads, rows=512):
    t = g.shape[0]
    rows = min(rows, t)
    assert t % rows == 0 and rows % CHUNK == 0
    row_lo = 0 if mode == "gdn" else 2 * n_heads
    return pl.pallas_call(
        functools.partial(_gates_kernel, mode=mode, n_heads=n_heads, rows=rows, row_lo=row_lo),
        grid=(t // rows,),
        in_specs=[
            pl.BlockSpec((rows, LANES), lambda i: (i, 0)),
            pl.BlockSpec((1, LANES), lambda i: (0, 0)),
            pl.BlockSpec((1, LANES), lambda i: (0, 0)),
        ],
        out_specs=[
            pl.BlockSpec((rows, LANES), lambda i: (i, 0)),
            pl.BlockSpec((rows // CHUNK, 2 * n_heads, CHUNK), lambda i: (i, 0, 0)),
        ],
        out_shape=[jax.ShapeDtypeStruct((t, LANES), F32), jax.ShapeDtypeStruct((t // CHUNK, 2 * n_heads, CHUNK), F32)],
        compiler_params=_cparams(("parallel",)),
        name="gate_prep_" + mode,
    )(g, p0, p1)


def _lane_vec(*parts):
    flat = jnp.concatenate([p.reshape(-1).astype(F32) for p in parts])
    return jnp.zeros((1, LANES), F32).at[0, :flat.shape[0]].set(flat)


def _col(tile, lane, j):
    return jnp.sum(jnp.where(lane == j, tile, 0.0), axis=1, keepdims=True)


def _tri_masks(direction):
    ri = lax.broadcasted_iota(jnp.int32, (CHUNK, CHUNK), 0)
    ci = lax.broadcasted_iota(jnp.int32, (CHUNK, CHUNK), 1)
    if direction == 0:
        return ri >= ci, ri > ci
    return ri <= ci, ri < ci


def _unit_tri_inverses(ms):
    ri = lax.broadcasted_iota(jnp.int32, (CHUNK, CHUNK), 0)
    ci = lax.broadcasted_iota(jnp.int32, (CHUNK, CHUNK), 1)
    eye = jnp.where(ri == ci, 1.0, 0.0)
    blk = ri // SUB == ci // SUB
    ds = [jnp.where(blk, m, 0.0) for m in ms]
    offs = [m - d for m, d in zip(ms, ds)]
    ps = [eye - d for d in ds]
    dps = ds
    for _ in range(3):
        dps = [_mm16(dp, dp) for dp in dps]
        ps = [p + _mm16(p, dp) for p, dp in zip(ps, dps)]
    ns = [_mm16(p, off) for p, off in zip(ps, offs)]
    n2s = [_mm16(n, n) for n in ns]
    qs = [(eye - n) + _mm16(eye - n, n2) for n, n2 in zip(ns, n2s)]
    return [_mm16(q, p) for q, p in zip(qs, ps)]


def _gdn_kernel(q_ref, k_ref, v_ref, z_ref, cwq_ref, cwk_ref, cwv_ref, gc_ref, gr_ref, gn_ref, o_ref,
                pad_ref, qn_ref, kn_ref, vn_ref, aq_ref, b_ref, egl_ref, oacc_ref,
                *, seq, n_heads, conv_rows, dk, group):
    h = pl.program_id(1)
    n_chunks = seq // CHUNK
    half = GDN_CONV // 2

    for src, cw_ref, dst, kind in ((q_ref, cwq_ref, qn_ref, "q"), (k_ref, cwk_ref, kn_ref, "k"), (v_ref, cwv_ref, vn_ref, "v")):
        pad_ref[0:8, :] = jnp.zeros((8, LANES), F32)
        pad_ref[seq + 8:seq + 16, :] = jnp.zeros((8, LANES), F32)
        pad_ref[8:seq + 8, :] = src[0].astype(F32)
        w = cw_ref[...]

        def conv_block(i, c, w=w, dst=dst, kind=kind):
            t0 = pl.multiple_of(i * conv_rows, conv_rows)
            acc = jnp.zeros((conv_rows, LANES), F32)
            for j in range(GDN_CONV):
                acc = acc + pad_ref[pl.ds(t0 + (8 - half + j), conv_rows), :] * w[j:j + 1, :]
            y = _silu(acc)
            if kind != "v":
                y = y * lax.rsqrt(jnp.sum(y * y, axis=-1, keepdims=True) + EPS)
            if kind == "q":
                y = y * (dk ** -0.5)
            dst[pl.ds(t0, conv_rows), :] = y
            return c

        lax.fori_loop(0, seq // conv_rows, conv_block, 0)

    oacc_ref[...] = jnp.zeros_like(oacc_ref)
    lane = lax.broadcasted_iota(jnp.int32, (CHUNK, LANES), 1)

    def prepare(g, carry):
        chunks = []
        for j in range(group):
            c = g * group + j
            r0 = pl.multiple_of(c * CHUNK, CHUNK)
            qc = qn_ref[pl.ds(r0, CHUNK), :]
            kc = kn_ref[pl.ds(r0, CHUNK), :]
            chunks.append(dict(c=c, r0=r0, tile=gc_ref[0, pl.ds(r0, CHUNK), :], qc=qc, kc=kc,
                               vc=vn_ref[pl.ds(r0, CHUNK), :], q16=qc.astype(BF16), k16=kc.astype(BF16)))
        for ch in chunks:
            ch["kk"] = _dot_nt(ch["k16"], ch["k16"])
        for ch in chunks:
            ch["qk"] = _dot_nt(ch["q16"], ch["k16"])
        chains = []
        for ch in chunks:
            kt = jnp.transpose(ch["kc"])
            for direction in range(2):
                jd = direction * n_heads + h
                incl, strict = _tri_masks(direction)
                gcol = _col(ch["tile"], lane, jd)
                bcol = _col(ch["tile"], lane, 2 * n_heads + jd)
                grow = gr_ref[0, ch["c"], pl.ds(jd, 1), :]
                glast = gcol[CHUNK - 1:CHUNK] if direction == 0 else gcol[0:1]
                gam = jnp.exp(jnp.where(incl, gcol - grow, -jnp.inf))
                eg = jnp.exp(gcol)
                rhs = jnp.concatenate([ch["vc"] * bcol, ch["kc"] * (bcol * eg)], axis=1)
                rhs_hi = rhs.astype(BF16)
                chains.append(dict(
                    ch=ch, direction=direction, m=jnp.where(strict, ch["kk"] * bcol * gam, 0.0),
                    rhs_hi=rhs_hi, rhs_lo=(rhs - rhs_hi.astype(F32)).astype(BF16),
                    lhs=jnp.concatenate([(kt * jnp.exp(glast - grow)).astype(BF16), (ch["qk"] * gam).astype(BF16)], axis=0),
                    qd=ch["qc"] * eg, egl=jnp.exp(glast)))
        tinvs = [t.astype(BF16) for t in _unit_tri_inverses([cn["m"] for cn in chains])]
        sols = [_dot(t, cn["rhs_hi"]) + _dot(t, cn["rhs_lo"]) for t, cn in zip(tinvs, chains)]
        for cn, sol in zip(chains, sols):
            uw = jnp.concatenate([sol[:, :LANES], -sol[:, LANES:]], axis=1).astype(BF16)
            cn["r"] = _dot(cn["lhs"], uw)
        for cn in chains:
            d, c, r = cn["direction"], cn["ch"]["c"], cn["r"]
            aq_ref[d, c, 0:dk, :] = r[:dk, LANES:].astype(BF16)
            aq_ref[d, c, dk:dk + CHUNK, :] = (cn["qd"] + r[dk:, LANES:]).astype(BF16)
            b_ref[d, c] = r[:dk, :LANES].astype(BF16)
            egl_ref[d, c] = jnp.broadcast_to(cn["egl"], (8, LANES))
            oacc_ref[pl.ds(cn["ch"]["r0"], CHUNK), :] += r[dk:, :LANES]
        return carry

    lax.fori_loop(0, n_chunks // group, prepare, 0)

    def scan_step(c, direction, state):
        r0 = pl.multiple_of(c * CHUNK, CHUNK)
        r = _dot(aq_ref[direction, c], state.astype(BF16))
        oacc_ref[pl.ds(r0, CHUNK), :] += r[dk:]
        return state * egl_ref[direction, c][0:1, :] + r[:dk] + b_ref[direction, c].astype(F32)

    def scan(i, carry):
        sf, sb = carry
        return scan_step(i, 0, sf), scan_step(n_chunks - 1 - i, 1, sb)

    zero = jnp.zeros((dk, LANES), F32)
    lax.fori_loop(0, n_chunks, scan, (zero, zero))

    def epilogue(i, c):
        t0 = pl.multiple_of(i * conv_rows, conv_rows)
        o = oacc_ref[pl.ds(t0, conv_rows), :]
        ms = jnp.mean(o * o, axis=-1, keepdims=True)
        y = o * lax.rsqrt(ms + EPS) * gn_ref[...]
        z = z_ref[0, pl.ds(t0, conv_rows), :].astype(F32)
        o_ref[0, pl.ds(t0, conv_rows), :] = (y * _silu(z)).astype(o_ref.dtype)
        return c

    lax.fori_loop(0, seq // conv_rows, epilogue, 0)


def gdn_mixer(proj, conv_w, gcol, grow, gdn_norm, *, batch, seq, n_heads, col0, conv_rows=256, group=8):
    hb = n_heads
    conv_rows = min(conv_rows, seq)
    n_chunks = seq // CHUNK
    dk = LANES
    group = math.gcd(group, n_chunks)

    def head_spec(group):
        return pl.BlockSpec((1, seq, LANES), lambda b, h, group=group: (b, 0, col0 + group * hb + h))

    def conv_spec(group):
        return pl.BlockSpec((GDN_CONV, LANES), lambda b, h, group=group: (0, group * hb + h))

    return pl.pallas_call(
        functools.partial(_gdn_kernel, seq=seq, n_heads=n_heads, conv_rows=conv_rows, dk=dk, group=group),
        grid=(batch, n_heads),
        in_specs=[
            head_spec(0), head_spec(1), head_spec(2), head_spec(3),
            conv_spec(0), conv_spec(1), conv_spec(2),
            pl.BlockSpec((1, seq, LANES), lambda b, h: (b, 0, 0)),
            pl.BlockSpec((1, n_chunks, 2 * n_heads, CHUNK), lambda b, h: (b, 0, 0, 0)),
            pl.BlockSpec((1, LANES), lambda b, h: (0, 0)),
        ],
        out_specs=pl.BlockSpec((1, seq, LANES), lambda b, h: (b, 0, h)),
        out_shape=jax.ShapeDtypeStruct((batch, seq, n_heads * LANES), BF16),
        scratch_shapes=[
            pltpu.VMEM((seq + 16, LANES), F32),
            pltpu.VMEM((seq, LANES), F32),
            pltpu.VMEM((seq, LANES), F32),
            pltpu.VMEM((seq, LANES), F32),
            pltpu.VMEM((2, n_chunks, dk + CHUNK, LANES), BF16),
            pltpu.VMEM((2, n_chunks, dk, LANES), BF16),
            pltpu.VMEM((2, n_chunks, 8, LANES), F32),
            pltpu.VMEM((seq, LANES), F32),
        ],
        compiler_params=_cparams(("parallel", "arbitrary")),
        name="gdn_mixer",
    )(proj, proj, proj, proj, conv_w, conv_w, conv_w,
      gcol.reshape(batch, seq, LANES), grow.reshape(batch, n_chunks, 2 * n_heads, CHUNK),
      gdn_norm.reshape(1, LANES).astype(F32))


def _mlstm_kernel(q_ref, k_ref, v_ref, og_ref, gc_ref, gr_ref, nrm_ref, o_ref,
                  hf_ref, hb_ref, st_ref, *, seq, n_heads, rows, dqk, dv):
    p = pl.program_id(1)
    n_chunks = seq // CHUNK
    lane = lax.broadcasted_iota(jnp.int32, (CHUNK, LANES), 1)
    ones = jnp.ones((CHUNK, dv), BF16)
    scale = dqk ** -0.5

    st_ref[...] = jnp.zeros_like(st_ref)

    def body(i, ms):
        chains = []
        for direction in range(2):
            c = i if direction == 0 else n_chunks - 1 - i
            r0 = pl.multiple_of(c * CHUNK, CHUNK)
            incl, _ = _tri_masks(direction)
            tile = gc_ref[0, pl.ds(r0, CHUNK), :]
            for hh in range(2):
                idx = direction * 2 + hh
                m = ms[idx]
                jd = direction * n_heads + 2 * p + hh
                bc = _col(tile, lane, jd)
                ac = _col(tile, lane, 2 * n_heads + jd)
                arow = gr_ref[0, c, pl.ds(jd, 1), :]
                blast = bc[CHUNK - 1:CHUNK] if direction == 0 else bc[0:1]
                dmat = jnp.where(incl, bc - arow, NEG)
                inter = bc + m
                mj = jnp.maximum(inter, jnp.max(dmat, axis=1, keepdims=True))
                wl = blast - ac
                m_new = jnp.maximum(blast + m, jnp.max(wl, axis=0, keepdims=True))
                k = k_ref[0, pl.ds(r0, CHUNK), hh * dqk:(hh + 1) * dqk]
                chains.append(dict(
                    idx=idx, r0=r0, hh=hh, dst=hf_ref if direction == 0 else hb_ref, mj=mj, m_new=m_new,
                    q=q_ref[0, pl.ds(r0, CHUNK), hh * dqk:(hh + 1) * dqk], k=k,
                    vaug=jnp.concatenate([v_ref[0, pl.ds(r0, CHUNK), hh * dv:(hh + 1) * dv], ones], axis=1),
                    gate=jnp.exp(dmat - mj) * scale, iw=jnp.exp(inter - mj), decay=jnp.exp(blast + m - m_new),
                    kw=(k.astype(F32) * (jnp.exp(wl - m_new) * scale)).astype(BF16), state=st_ref[idx]))
        for cn in chains:
            cn["qk"] = _dot_nt(cn["q"], cn["k"])
        for cn in chains:
            cn["qc"] = _dot(cn["q"], cn["state"].astype(BF16))
        for cn in chains:
            cn["upd"] = _dot_tn(cn["kw"], cn["vaug"])
        for cn in chains:
            cn["sv"] = _dot((cn["qk"] * cn["gate"]).astype(BF16), cn["vaug"])
        for cn in chains:
            num = cn["iw"] * cn["qc"] + cn["sv"]
            den = num[:, dv:dv + 1]
            hout = num[:, :dv] / jnp.maximum(jnp.abs(den), jnp.exp(-cn["mj"]))
            cn["dst"][pl.ds(cn["r0"], CHUNK), cn["hh"] * dv:(cn["hh"] + 1) * dv] = hout
            st_ref[cn["idx"]] = cn["decay"] * cn["state"] + cn["upd"]
        return tuple(cn["m_new"] for cn in chains)

    zero = jnp.zeros((1, 1), F32)
    lax.fori_loop(0, n_chunks, body, (zero, zero, zero, zero))

    def epilogue(i, c):
        t0 = pl.multiple_of(i * rows, rows)
        g = nrm_ref[0]
        og = og_ref[0, pl.ds(t0, rows), :].astype(F32)
        for hh in range(2):
            sl = slice(hh * dv, (hh + 1) * dv)
            x = hf_ref[pl.ds(t0, rows), sl] + hb_ref[pl.ds(t0, rows), sl]
            ms = jnp.mean(x * x, axis=-1, keepdims=True)
            y = x * lax.rsqrt(ms + EPS) * g[:, sl]
            o_ref[0, pl.ds(t0, rows), sl] = (y * jax.nn.sigmoid(og[:, sl])).astype(o_ref.dtype)
        return c

    lax.fori_loop(0, seq // rows, epilogue, 0)


def mlstm_mixer(proj, gcol, grow, mlstm_norm, *, batch, seq, n_heads, q_blk, k_blk, v_blk, o_blk, rows=256):
    dqk, dv = MLSTM_DQK, MLSTM_DV
    n_chunks = seq // CHUNK
    rows = min(rows, seq)
    npair = n_heads // 2
    return pl.pallas_call(
        functools.partial(_mlstm_kernel, seq=seq, n_heads=n_heads, rows=rows, dqk=dqk, dv=dv),
        grid=(batch, npair),
        in_specs=[
            pl.BlockSpec((1, seq, 2 * dqk), lambda b, p: (b, 0, q_blk + p)),
            pl.BlockSpec((1, seq, 2 * dqk), lambda b, p: (b, 0, k_blk + p)),
            pl.BlockSpec((1, seq, 2 * dv), lambda b, p: (b, 0, v_blk + p)),
            pl.BlockSpec((1, seq, 2 * dv), lambda b, p: (b, 0, o_blk + p)),
            pl.BlockSpec((1, seq, LANES), lambda b, p: (b, 0, 0)),
            pl.BlockSpec((1, n_chunks, 2 * n_heads, CHUNK), lambda b, p: (b, 0, 0, 0)),
            pl.BlockSpec((1, 1, 2 * dv), lambda b, p: (p, 0, 0)),
        ],
        out_specs=pl.BlockSpec((1, seq, 2 * dv), lambda b, p: (b, 0, p)),
        out_shape=jax.ShapeDtypeStruct((batch, seq, n_heads * dv), BF16),
        scratch_shapes=[
            pltpu.VMEM((seq, 2 * dv), F32),
            pltpu.VMEM((seq, 2 * dv), F32),
            pltpu.VMEM((4, dqk, 2 * dv), F32),
        ],
        compiler_params=_cparams(("parallel", "arbitrary")),
        name="mlstm_mixer",
    )(proj, proj, proj, proj, gcol.reshape(batch, seq, LANES), grow.reshape(batch, n_chunks, 2 * n_heads, CHUNK),
      mlstm_norm.reshape(npair, 1, 2 * dv).astype(F32))


def _dsw_kernel(slope_ref, q_ref, k_ref, v_ref, o_ref, qf_ref, kf_ref, vf_ref, acc_ref, m_ref, l_ref,
                *, seq, pairs, qb, group, rows):
    slope = slope_ref[pl.program_id(1)]
    scale = LANES ** -0.5

    def to_f32(i, c):
        sl = pl.ds(pl.multiple_of(i * rows, rows), rows)
        qf_ref[sl, :] = q_ref[0, sl, :].astype(F32)
        kf_ref[sl, :] = k_ref[0, sl, :].astype(F32)
        vf_ref[sl, :] = v_ref[0, sl, :].astype(F32)
        return c

    lax.fori_loop(0, seq // rows, to_f32, 0)

    for branch, (window, dil) in enumerate(pairs):
        u_len = seq // dil
        half = window // (2 * dil)
        kb = qb + 2 * half
        n_blk = u_len // qb
        rel = lax.broadcasted_iota(jnp.int32, (qb, kb), 1) - lax.broadcasted_iota(jnp.int32, (qb, kb), 0)

        def blocks(g, c, dil=dil, half=half, kb=kb, n_blk=n_blk, u_len=u_len, rel=rel, branch=branch):
            work = []
            for j in range(group):
                bidx = g * group + j
                cls = bidx // n_blk
                u0 = (bidx % n_blk) * qb
                ks = jnp.clip(u0 - half, 0, u_len - kb)
                work.append(dict(
                    rows=pl.ds(cls + dil * u0, qb, stride=dil), dist=jnp.abs(rel + (ks - u0)),
                    q=qf_ref[pl.ds(cls + dil * u0, qb, stride=dil), :].astype(BF16),
                    k=kf_ref[pl.ds(cls + dil * ks, kb, stride=dil), :].astype(BF16),
                    v=vf_ref[pl.ds(cls + dil * ks, kb, stride=dil), :].astype(BF16)))
            for w in work:
                w["s"] = _dot_nt(w["q"], w["k"])
            for w in work:
                s = w["s"] * scale - (slope * float(dil)) * w["dist"].astype(F32)
                s = jnp.where(w["dist"] <= half, s, NEG)
                w["mx"] = jnp.max(s, axis=-1, keepdims=True)
                p = jnp.exp(s - w["mx"])
                w["den"] = jnp.sum(p, axis=-1, keepdims=True)
                w["p"] = p.astype(BF16)
            for w in work:
                w["o"] = _dot(w["p"], w["v"])
            for w in work:
                mx = jnp.broadcast_to(w["mx"], (qb, LANES))
                den = jnp.broadcast_to(w["den"], (qb, LANES))
                if branch == 0:
                    acc_ref[w["rows"], :] = w["o"]
                    m_ref[w["rows"], :] = mx
                    l_ref[w["rows"], :] = den
                else:
                    m_old = m_ref[w["rows"], :]
                    m_new = jnp.maximum(m_old, mx)
                    a = jnp.exp(m_old - m_new)
                    b = jnp.exp(mx - m_new)
                    acc_ref[w["rows"], :] = acc_ref[w["rows"], :] * a + w["o"] * b
                    l_ref[w["rows"], :] = l_ref[w["rows"], :] * a + den * b
                    m_ref[w["rows"], :] = m_new
            return c

        lax.fori_loop(0, dil * n_blk // group, blocks, 0)

    def finish(i, c):
        sl = pl.ds(pl.multiple_of(i * rows, rows), rows)
        o_ref[0, sl, :] = (acc_ref[sl, :] / l_ref[sl, :]).astype(o_ref.dtype)
        return c

    lax.fori_loop(0, seq // rows, finish, 0)


def dsw_attention(proj, slopes, *, batch, seq, n_heads, q_blk, k_blk, v_blk, pairs, qb=128, group=4, rows=512):
    rows = min(rows, seq)
    for window, dil in pairs:
        u_len = seq // dil
        assert seq % dil == 0 and u_len % qb == 0 and u_len >= qb + window // dil
        assert (dil * (u_len // qb)) % group == 0

    def spec(blk):
        return pl.BlockSpec((1, seq, LANES), lambda b, h, s, blk=blk: (b, 0, blk + h))

    return pl.pallas_call(
        functools.partial(_dsw_kernel, seq=seq, pairs=pairs, qb=qb, group=group, rows=rows),
        grid_spec=pltpu.PrefetchScalarGridSpec(
            num_scalar_prefetch=1,
            grid=(batch, n_heads),
            in_specs=[spec(q_blk), spec(k_blk), spec(v_blk)],
            out_specs=pl.BlockSpec((1, seq, LANES), lambda b, h, s: (b, 0, h)),
            scratch_shapes=[pltpu.VMEM((seq, LANES), F32) for _ in range(6)],
        ),
        out_shape=jax.ShapeDtypeStruct((batch, seq, n_heads * LANES), BF16),
        compiler_params=_cparams(("parallel", "arbitrary")),
        name="dsw_attention",
    )(slopes, proj, proj, proj)


def _mla_kernel(q_ref, tq_ref, kn_ref, v_ref, kr_ref, krr_ref, ck_ref, sk_ref, o_ref, k_scr, v_scr, *, seq, kvb):
    @pl.when(pl.program_id(2) == 0)
    def _():
        k_scr[:, :LANES] = kn_ref[0]
        rk = kr_ref[0].astype(F32) * ck_ref[...] + krr_ref[0].astype(F32) * sk_ref[...]
        k_scr[:, LANES:] = rk.astype(BF16)
        v_scr[:, :LANES] = v_ref[0]
        v_scr[:, LANES:] = jnp.ones((seq, LANES), BF16)

    q = (q_ref[0].astype(F32) * tq_ref[...]).astype(BF16)
    mx = jnp.full((q.shape[0], 1), -jnp.inf, F32)
    acc = jnp.zeros((q.shape[0], 2 * LANES), F32)
    for b in range(seq // kvb):
        s = _dot_nt(q, k_scr[b * kvb:(b + 1) * kvb, :])
        mx_new = jnp.maximum(mx, jnp.max(s, axis=-1, keepdims=True))
        p = jnp.exp2(s - mx_new).astype(BF16)
        acc = acc * jnp.exp2(mx - mx_new) + _dot(p, v_scr[b * kvb:(b + 1) * kvb, :])
        mx = mx_new
    o_ref[0] = (acc[:, :LANES] / acc[:, LANES:LANES + 1]).astype(o_ref.dtype)


def mla_attention(q_up, kv_up, proj, tq_tab, ck_tab, sk_tab, *, batch, seq, n_heads, kr_blk, krr_blk, tq=512, kvb=512):
    tq = min(tq, seq)
    kvb = min(kvb, seq)
    return pl.pallas_call(
        functools.partial(_mla_kernel, seq=seq, kvb=kvb),
        grid=(batch, n_heads, seq // tq),
        in_specs=[
            pl.BlockSpec((1, tq, 2 * LANES), lambda b, h, i: (b, i, h)),
            pl.BlockSpec((tq, 2 * LANES), lambda b, h, i: (i, 0)),
            pl.BlockSpec((1, seq, LANES), lambda b, h, i: (b, 0, 2 * h)),
            pl.BlockSpec((1, seq, LANES), lambda b, h, i: (b, 0, 2 * h + 1)),
            pl.BlockSpec((1, seq, LANES), lambda b, h, i: (b, 0, kr_blk)),
            pl.BlockSpec((1, seq, LANES), lambda b, h, i: (b, 0, krr_blk)),
            pl.BlockSpec((seq, LANES), lambda b, h, i: (0, 0)),
            pl.BlockSpec((seq, LANES), lambda b, h, i: (0, 0)),
        ],
        out_specs=pl.BlockSpec((1, tq, LANES), lambda b, h, i: (b, i, h)),
        out_shape=jax.ShapeDtypeStruct((batch, seq, n_heads * LANES), BF16),
        scratch_shapes=[pltpu.VMEM((seq, 2 * LANES), BF16), pltpu.VMEM((seq, 2 * LANES), BF16)],
        compiler_params=_cparams(("parallel", "parallel", "arbitrary")),
        name="mla_attention",
    )(q_up, tq_tab, kv_up, kv_up, proj, proj, ck_tab, sk_tab)


def _rot_cols(w):
    half = w.shape[-1] // 2
    return jnp.concatenate([-w[..., half:], w[..., :half]], axis=-1)


def layer_even(h, batch, seq, mix_norm, w_in, conv_w, a_log, dt_bias, gdn_norm, w_out, ffn_norm, w_gate, w_up, w_down,
               *, tm=512, tm_in=1024, tf=1024):
    d = h.shape[1]
    nh = GDN_HEADS
    qk = nh * LANES
    gate0 = 4 * qk
    att0 = gate0 + 4 * nh
    w_gates = jnp.zeros((d, LANES), F32).at[:, :4 * nh].set(w_in[:, gate0:att0]).astype(BF16)
    proj_a = norm_matmul(h, mix_norm, w_in[:, :gate0].astype(BF16), out_dtype=BF16, tm=tm_in, name="l0_in_proj_gdn")
    proj_b = norm_matmul(h, mix_norm, w_in[:, att0:].astype(BF16), out_dtype=BF16, tm=tm_in, name="l0_in_proj_dsw")
    graw = norm_matmul(h, mix_norm, w_gates, out_dtype=F32, tm=tm_in, name="l0_gate_proj")
    gcol, grow = gate_prep(graw, _lane_vec(a_log), _lane_vec(dt_bias), mode="gdn", n_heads=nh)
    o_a = gdn_mixer(proj_a.reshape(batch, seq, gate0), conv_w.astype(F32), gcol, grow, gdn_norm,
                    batch=batch, seq=seq, n_heads=nh, col0=0)
    slopes = (2.0 ** (-8.0 * jnp.arange(1, DSW_HEADS + 1, dtype=F32) / DSW_HEADS)).astype(F32)
    o_b = dsw_attention(proj_b.reshape(batch, seq, proj_b.shape[1]), slopes, batch=batch, seq=seq, n_heads=DSW_HEADS,
                        q_blk=0, k_blk=DSW_HEADS, v_blk=2 * DSW_HEADS, pairs=DSW_PAIRS)
    na = nh * LANES
    h = proj_residual(o_a.reshape(batch * seq, na), o_b.reshape(batch * seq, -1), w_out[:na].astype(BF16),
                      w_out[na:].astype(BF16), h,
                      tm=tm, name="l0_out_proj")
    return ffn_residual(h, ffn_norm, _tile_cols(w_gate, tf), _tile_cols(w_up, tf), w_down.astype(BF16), tm=tm)


def layer_odd_mixers(h, batch, seq, mix_norm, w_in, q_norm, kv_norm, w_uq, w_ukv, ig_bias, fg_bias, mlstm_norm, w_out,
                     *, tm=512, tm_in=1024):
    d = h.shape[1]
    nh = MLSTM_HEADS
    sizes = (MLA_Q_RANK, MLA_KV_RANK, MLA_ROPE, nh * MLSTM_DQK, nh * MLSTM_DQK, nh * MLSTM_DV, nh * MLSTM_DV, 2 * nh, 2 * nh)
    cuts = np.cumsum((0,) + sizes)
    w_cq, w_ckv, w_kr, w_mq, w_mk, w_mv, w_mo, w_mi, w_mf = (w_in[:, cuts[i]:cuts[i + 1]] for i in range(9))
    w_krot = _rot_cols(w_kr)
    w_main = jnp.concatenate([w_cq, w_kr, w_kr, w_krot, w_krot, w_ckv, w_mq, w_mk, w_mv, w_mo], axis=1).astype(BF16)
    w_gates = jnp.zeros((d, LANES), F32).at[:, :2 * nh].set(w_mf).at[:, 2 * nh:4 * nh].set(w_mi).astype(BF16)
    proj = norm_matmul(h, mix_norm, w_main, out_dtype=BF16, tm=tm_in, name="l1_in_proj")
    graw = norm_matmul(h, mix_norm, w_gates, out_dtype=F32, tm=tm_in, name="l1_gate_proj")
    gcol, grow = gate_prep(graw, _lane_vec(fg_bias), _lane_vec(jnp.zeros((2 * nh,), F32), ig_bias), mode="mlstm", n_heads=nh)
    width = proj.shape[1]
    proj3 = proj.reshape(batch, seq, width)

    hq = MLA_HEADS
    wq = w_uq.reshape(MLA_Q_RANK, hq, MLA_NOPE + MLA_ROPE)
    wq_rope = wq[:, :, MLA_NOPE:]
    wq_all = jnp.concatenate([wq[:, :, :MLA_NOPE], wq_rope, _rot_cols(wq_rope)], axis=-1).reshape(MLA_Q_RANK, hq * 2 * LANES)
    q_up = norm_matmul(proj, q_norm, wq_all.astype(BF16), out_dtype=BF16, k_block=0, k_width=MLA_Q_RANK, tm=tm_in, name="mla_q_up")
    kv_blk = (MLA_Q_RANK + 2 * LANES) // MLA_KV_RANK
    kv_up = norm_matmul(proj, kv_norm, w_ukv.astype(BF16), out_dtype=BF16, k_block=kv_blk, k_width=MLA_KV_RANK, tm=tm_in, name="mla_kv_up")
    pos = jnp.arange(seq, dtype=F32)
    freqs = ROPE_THETA ** (-jnp.arange(0, MLA_ROPE, 2, dtype=F32) / MLA_ROPE)
    ang = pos[:, None] * freqs[None, :]
    cos, sin = jnp.cos(ang), jnp.sin(ang)
    scale = LOG2E * (MLA_NOPE + MLA_ROPE) ** -0.5
    tq_tab = scale * jnp.concatenate([jnp.ones((seq, MLA_NOPE), F32), cos, cos, sin, sin], axis=1)
    ck_tab = jnp.concatenate([cos] * 4, axis=1)
    sk_tab = jnp.concatenate([sin] * 4, axis=1)
    kr_blk = MLA_Q_RANK // LANES
    o_c = mla_attention(q_up.reshape(batch, seq, -1), kv_up.reshape(batch, seq, -1), proj3, tq_tab, ck_tab, sk_tab,
                        batch=batch, seq=seq, n_heads=hq, kr_blk=kr_blk, krr_blk=kr_blk + 1)

    mq0 = kr_blk + 2 + MLA_KV_RANK // LANES
    mk0 = mq0 + nh * MLSTM_DQK // LANES
    mv0 = mk0 + nh * MLSTM_DQK // LANES
    mo0 = mv0 + nh * MLSTM_DV // LANES
    o_d = mlstm_mixer(proj3, gcol, grow, mlstm_norm, batch=batch, seq=seq, n_heads=nh,
                      q_blk=mq0, k_blk=mk0, v_blk=mv0 // 2, o_blk=mo0 // 2)
    nc = hq * MLA_V
    return proj_residual(o_c.reshape(batch * seq, nc), o_d.reshape(batch * seq, -1), w_out[:nc].astype(BF16),
                         w_out[nc:].astype(BF16), h, tm=tm, name="l1_out_proj")


def _tile_cols(w, tf):
    *lead, k, n = w.shape
    tf = min(tf, n)
    return jnp.moveaxis(w.astype(BF16).reshape(*lead, k, n // tf, tf), -2, -3)


def moe_block(h, ffn_norm, w_router, b_router, we_gate, we_up, we_down, final_norm, *, tm=896, tf=512, tm_tok=512):
    t = h.shape[0]
    n_experts = we_gate.shape[0]
    tm = min(tm, t)
    xn, info = router(h, ffn_norm, w_router, b_router, tm=tm_tok)
    te, n_active, n_valid, tok_of_slot, dst_of_slot = moe_dispatch_tables(info, tm=tm, n_experts=n_experts)
    y = moe_experts(xn, te, n_active, n_valid, tok_of_slot, dst_of_slot, _tile_cols(we_gate, tf), _tile_cols(we_up, tf),
                    we_down.astype(BF16), tm=tm)
    return final_combine(h, y, info, final_norm, tm=tm_tok)


def kernel(x, even_mix_norm, even_w_in, even_conv_w, even_a_log, even_dt_bias, even_gdn_norm, even_w_out,
           even_ffn_norm, even_w_gate, even_w_up, even_w_down, odd_mix_norm, odd_w_in, odd_q_norm, odd_kv_norm,
           odd_w_uq, odd_w_ukv, odd_ig_bias, odd_fg_bias, odd_mlstm_norm, odd_w_out, odd_ffn_norm,
           odd_w_router, odd_b_router, odd_we_gate, odd_we_up, odd_we_down, final_norm):
    batch, seq, d = x.shape
    h = x.reshape(batch * seq, d)
    h = layer_even(h, batch, seq, even_mix_norm[0], even_w_in[0], even_conv_w[0], even_a_log[0], even_dt_bias[0],
                   even_gdn_norm[0], even_w_out[0], even_ffn_norm[0], even_w_gate[0], even_w_up[0], even_w_down[0])
    h = layer_odd_mixers(h, batch, seq, odd_mix_norm[0], odd_w_in[0], odd_q_norm[0], odd_kv_norm[0], odd_w_uq[0],
                         odd_w_ukv[0], odd_ig_bias[0], odd_fg_bias[0], odd_mlstm_norm[0], odd_w_out[0])
    out = moe_block(h, odd_ffn_norm[0], odd_w_router[0], odd_b_router[0], odd_we_gate[0], odd_we_up[0],
                    odd_we_down[0], final_norm)
    return out.reshape(batch, seq, d)
```
